```python
import math
import jax, jax.numpy as jnp
from jax import lax
import numpy as np

D_MODEL = 1024
BATCH = 8
SEQ = 2048
DEPTH = 4
DEC_BATCH = 128
DEC_SEQ = 4
PAST_LEN = 16384
PAGE_SIZE = 128

A_WIDTH = D_MODEL // 2
A_HEAD_DIM = 64
A_HEADS = A_WIDTH // A_HEAD_DIM
A_DECAY_LORA = 64
A_AAA_LORA = 64
A_MV_LORA = 32
A_GATE_LORA = 128
GN_EPS_A = 64e-5
B_WIDTH = D_MODEL - A_WIDTH
B_HEADS = 4
B_DV = B_WIDTH // B_HEADS
B_DK = B_DV // 2
B_KW = B_HEADS * B_DK
B_GATE_LORA = 16
GLA_LOGIT_NORM = 16.0
GLA_CHUNK = 64
RMS_EPS = 1e-5
D_FF = 2816
CONV_W = 3
ALPHA = (2 * DEPTH) ** 0.25
BETA = (8 * DEPTH) ** -0.25
LN_EPS = 1e-5
SHIFT_W = 3 * A_WIDTH + A_DECAY_LORA + A_AAA_LORA + A_GATE_LORA
GLA_IN_W = 2 * B_KW + 2 * B_WIDTH + B_GATE_LORA
IN_W = SHIFT_W + GLA_IN_W

kernel_name = 'hybrid_rwkv7_gla_convglu_step'


def split_cols(t, sizes):
    idx = np.cumsum(sizes)[:-1].tolist()
    return jnp.split(t, idx, axis=-1)


def layer_norm(x, g, b):
    xf = x.astype(jnp.float32)
    mu = jnp.mean(xf, -1, keepdims=True)
    var = jnp.mean(jnp.square(xf - mu), -1, keepdims=True)
    return ((xf - mu) * lax.rsqrt(var + LN_EPS) * g + b).astype(x.dtype)


def rwkv7_group(pa, pa_prev, S0, v_first, vmix, mu, w0, w_up, a0, a_up, g_up, k_k, k_a, r_k, lnx_g, lnx_b):
    f32 = jnp.float32
    pa = pa.astype(f32)
    B, T, _ = pa.shape
    prev = jnp.concatenate([pa_prev.astype(f32), pa[:, :-1]], axis=1)
    xs = pa + (prev - pa) * mu
    r, k, v, w_lo, a_lo, g_lo = split_cols(xs, [A_WIDTH, A_WIDTH, A_WIDTH, A_DECAY_LORA, A_AAA_LORA, A_GATE_LORA])
    w = -jax.nn.softplus(-(w0 + jnp.tanh(w_lo) @ w_up)) - 0.5
    decay = jnp.exp(-jnp.exp(w))
    a = jax.nn.sigmoid(a0 + a_lo @ a_up)
    g = jax.nn.sigmoid(g_lo) @ g_up
    if vmix is None:
        v_first = v
    else:
        v0, v1, v2 = vmix
        v = v + (v_first - v) * jax.nn.sigmoid(v0 + (v @ v1) @ v2)
    heads = lambda t: t.reshape(B, T, A_HEADS, A_HEAD_DIM)
    kk = heads(k * k_k)
    kk = kk * lax.rsqrt(jnp.maximum(jnp.sum(kk * kk, -1, keepdims=True), 1e-24))
    k = k * (1.0 + (a - 1.0) * k_a)
    rh, kh, vh, dh, ah = heads(r), heads(k), heads(v), heads(decay), heads(a)

    def step(S, inp):
        r_t, d_t, k_t, v_t, kk_t, a_t = inp
        sa = jnp.einsum('bhvk,bhk->bhv', S, -kk_t)
        S = (S * d_t[:, :, None, :] + sa[..., None] * (kk_t * a_t)[:, :, None, :]
             + v_t[..., None] * k_t[:, :, None, :])
        return S, jnp.einsum('bhvk,bhk->bhv', S, r_t)

    tm = lambda t: jnp.moveaxis(t, 1, 0)
    S, o = lax.scan(step, S0.astype(f32), (tm(rh), tm(dh), tm(kh), tm(vh), tm(kk), tm(ah)))
    o = jnp.moveaxis(o, 0, 1)
    m = jnp.mean(o, -1, keepdims=True)
    var = jnp.mean(jnp.square(o - m), -1, keepdims=True)
    o = ((o - m) * lax.rsqrt(var + GN_EPS_A)).reshape(B, T, A_WIDTH) * lnx_g + lnx_b
    bonus = (jnp.sum(rh * kh * r_k, -1, keepdims=True) * vh).reshape(B, T, A_WIDTH)
    return (o + bonus) * g, S, pa[:, -1:], v_first


def gla_chunked(q, k, v, gk, S0):
    B, T, H, K = q.shape
    V = v.shape[-1]
    C = math.gcd(T, GLA_CHUNK)
    NC = T // C
    chunks = lambda t: t.reshape(B, NC, C, H, t.shape[-1]).swapaxes(0, 1)
    mask = jnp.tril(jnp.ones((C, C), dtype=bool))

    def step(S, inp):
        qc, kc, vc, gc = inp
        b = jnp.cumsum(gc, axis=1)
        qt = qc * jnp.exp(b)
        kt = kc * jnp.exp(-b)
        A = jnp.where(mask, jnp.einsum('bthk,bshk->bhts', qt, kt), 0.0)
        o = jnp.einsum('bhts,bshv->bthv', A, vc) + jnp.einsum('bthk,bhkv->bthv', qt, S)
        b_last = b[:, -1]
        kd = kc * jnp.exp(b_last[:, None] - b)
        S = S * jnp.exp(b_last)[..., None] + jnp.einsum('bshk,bshv->bhkv', kd, vc)
        return S, o

    S, o = lax.scan(step, S0, (chunks(q), chunks(k), chunks(v), chunks(gk)))
    return o.swapaxes(0, 1).reshape(B, T, H, V), S


def gla_group(pg, S0, f_up, f_bias, norm_g):
    f32 = jnp.float32
    pg = pg.astype(f32)
    B, T, _ = pg.shape
    q, k, v, g, f_lo = split_cols(pg, [B_KW, B_KW, B_WIDTH, B_WIDTH, B_GATE_LORA])
    gk = jax.nn.log_sigmoid(f_lo @ f_up + f_bias) / GLA_LOGIT_NORM
    hk = lambda t: t.reshape(B, T, B_HEADS, B_DK)
    o, S = gla_chunked(hk(q) * B_DK ** -0.5, hk(k), v.reshape(B, T, B_HEADS, B_DV), hk(gk), S0.astype(f32))
    o = o * lax.rsqrt(jnp.mean(o * o, -1, keepdims=True) + RMS_EPS) * norm_g
    return o.reshape(B, T, B_WIDTH) * jax.nn.silu(g), S


def conv_glu_ffn(x, conv_past, w_up, conv_w, conv_b, w_down):
    T = x.shape[1]
    gate, val = split_cols(x @ w_up, [D_FF, D_FF])
    padded = jnp.concatenate([conv_past.astype(gate.dtype), gate], axis=1)
    acc = conv_b
    for j in range(CONV_W):
        acc = acc + padded[:, j:j + T] * conv_w[j]
    h = jax.nn.gelu(acc, approximate=False) * val
    return h @ w_down, padded[:, T:]


def trunk(x, st_rwkv, st_shift, st_gla, st_conv, P):
    new_rwkv, new_shift, new_gla, new_conv = [], [], [], []
    v_first = None
    for l in range(DEPTH):
        proj = x @ P['w_in'][l]
        vmix = None if l == 0 else (P['vres_bias'][l - 1], P['vres_down'][l - 1], P['vres_up'][l - 1])
        oa, s_a, sh, v_first = rwkv7_group(
            proj[..., :SHIFT_W], st_shift[l], st_rwkv[l], v_first, vmix, P['tok_mu'][l],
            P['w0'][l], P['w_lora_up'][l], P['a0'][l], P['a_lora_up'][l], P['g_lora_up'][l],
            P['k_k'][l], P['k_a'][l], P['r_k'][l], P['lnx_g'][l], P['lnx_b'][l])
        ob, s_b = gla_group(proj[..., SHIFT_W:], st_gla[l], P['gla_f_up'][l], P['gla_f_bias'][l], P['gla_norm_g'][l])
        mix = jnp.concatenate([oa, ob], axis=-1).astype(x.dtype) @ P['w_out'][l]
        x = layer_norm(ALPHA * x + mix, P['ln1_g'][l], P['ln1_b'][l])
        f, cv = conv_glu_ffn(x, st_conv[l], P['w_up'][l], P['conv_w'][l], P['conv_b'][l], P['w_down'][l])
        x = layer_norm(ALPHA * x + f, P['ln2_g'][l], P['ln2_b'][l])
        new_rwkv.append(s_a)
        new_shift.append(sh)
        new_gla.append(s_b)
        new_conv.append(cv)
    return (x, jnp.stack(new_rwkv).astype(st_rwkv.dtype), jnp.stack(new_shift).astype(st_shift.dtype),
            jnp.stack(new_gla).astype(st_gla.dtype), jnp.stack(new_conv).astype(st_conv.dtype))


def setup_inputs(seed: int = 0) -> dict:
    key = jax.random.key(seed)
    ks = iter(jax.random.split(key, 40))
    f32 = jnp.float32
    nrm = lambda shape, s: jax.random.normal(next(ks), shape, f32) * s
    L = DEPTH
    x_prompt = nrm((BATCH, SEQ, D_MODEL), 1.0)
    x_sample = nrm((DEC_BATCH, DEC_SEQ, D_MODEL), 1.0)
    state_rwkv = nrm((L, DEC_BATCH, A_HEADS, A_HEAD_DIM, A_HEAD_DIM), 0.3)
    state_shift = nrm((L, DEC_BATCH, 1, SHIFT_W), 1.0)
    state_gla = nrm((L, DEC_BATCH, B_HEADS, B_DK, B_DV), 0.3)
    state_conv = nrm((L, DEC_BATCH, CONV_W - 1, D_FF), BETA)
    col_scale = np.ones(IN_W, np.float32)
    col_scale[2 * A_WIDTH:3 * A_WIDTH] = BETA
    gv0 = SHIFT_W + 2 * B_KW
    col_scale[gv0:gv0 + B_WIDTH] = BETA
    w_in = nrm((L, D_MODEL, IN_W), D_MODEL ** -0.5) * jnp.asarray(col_scale)
    tok_mu = jax.random.uniform(next(ks), (L, SHIFT_W), f32)
    w0 = jax.random.uniform(next(ks), (L, A_WIDTH), f32, -6.5, -1.5)
    w_lora_up = nrm((L, A_DECAY_LORA, A_WIDTH), 0.1)
    a0 = nrm((L, A_WIDTH), 0.1)
    a_lora_up = nrm((L, A_AAA_LORA, A_WIDTH), 0.1)
    g_lora_up = nrm((L, A_GATE_LORA, A_WIDTH), A_GATE_LORA ** -0.5)
    k_k = 0.85 + nrm((L, A_WIDTH), 0.02)
    k_a = 1.0 + nrm((L, A_WIDTH), 0.02)
    r_k = nrm((L, A_HEADS, A_HEAD_DIM), 0.1)
    lnx_g = 1.0 + nrm((L, A_WIDTH), 0.02)
    lnx_b = nrm((L, A_WIDTH), 0.02)
    vres_bias = 1.0 + nrm((L - 1, A_WIDTH), 0.1)
    vres_down = nrm((L - 1, A_WIDTH, A_MV_LORA), A_WIDTH ** -0.5)
    vres_up = nrm((L - 1, A_MV_LORA, A_WIDTH), 0.1)
    gla_f_up = nrm((L, B_GATE_LORA, B_KW), B_GATE_LORA ** -0.5)
    gla_f_bias = nrm((L, B_KW), 0.5)
    gla_norm_g = 1.0 + nrm((L, B_DV), 0.02)
    w_out = nrm((L, D_MODEL, D_MODEL), D_MODEL ** -0.5 * BETA)
    ln1_g = 1.0 + nrm((L, D_MODEL), 0.02)
    ln1_b = nrm((L, D_MODEL), 0.02)
    w_up = nrm((L, D_MODEL, 2 * D_FF), D_MODEL ** -0.5 * BETA)
    conv_w = nrm((L, CONV_W, D_FF), CONV_W ** -0.5)
    conv_b = nrm((L, D_FF), 0.02)
    w_down = nrm((L, D_FF, D_MODEL), D_FF ** -0.5 * BETA)
    ln2_g = 1.0 + nrm((L, D_MODEL), 0.02)
    ln2_b = nrm((L, D_MODEL), 0.02)
    return {'x_prompt': x_prompt, 'x_sample': x_sample, 'state_rwkv': state_rwkv,
            'state_shift': state_shift, 'state_gla': state_gla, 'state_conv': state_conv,
            'w_in': w_in, 'tok_mu': tok_mu, 'w0': w0, 'w_lora_up': w_lora_up, 'a0': a0,
            'a_lora_up': a_lora_up, 'g_lora_up': g_lora_up, 'k_k': k_k, 'k_a': k_a, 'r_k': r_k,
            'lnx_g': lnx_g, 'lnx_b': lnx_b, 'vres_bias': vres_bias, 'vres_down': vres_down,
            'vres_up': vres_up, 'gla_f_up': gla_f_up, 'gla_f_bias': gla_f_bias,
            'gla_norm_g': gla_norm_g, 'w_out': w_out, 'ln1_g': ln1_g, 'ln1_b': ln1_b,
            'w_up': w_up, 'conv_w': conv_w, 'conv_b': conv_b, 'w_down': w_down,
            'ln2_g': ln2_g, 'ln2_b': ln2_b}


def reference(x_prompt, x_sample, state_rwkv, state_shift, state_gla, state_conv,
              w_in, tok_mu, w0, w_lora_up, a0, a_lora_up, g_lora_up, k_k, k_a, r_k,
              lnx_g, lnx_b, vres_bias, vres_down, vres_up, gla_f_up, gla_f_bias, gla_norm_g,
              w_out, ln1_g, ln1_b, w_up, conv_w, conv_b, w_down, ln2_g, ln2_b):
    P = dict(w_in=w_in, tok_mu=tok_mu, w0=w0, w_lora_up=w_lora_up, a0=a0, a_lora_up=a_lora_up,
             g_lora_up=g_lora_up, k_k=k_k, k_a=k_a, r_k=r_k, lnx_g=lnx_g, lnx_b=lnx_b,
             vres_bias=vres_bias, vres_down=vres_down, vres_up=vres_up, gla_f_up=gla_f_up,
             gla_f_bias=gla_f_bias, gla_norm_g=gla_norm_g, w_out=w_out, ln1_g=ln1_g, ln1_b=ln1_b,
             w_up=w_up, conv_w=conv_w, conv_b=conv_b, w_down=w_down, ln2_g=ln2_g, ln2_b=ln2_b)
    zeros = lambda s: jnp.zeros((DEPTH, BATCH) + s.shape[2:], s.dtype)
    y_prompt, rwkv_p, shift_p, gla_p, conv_p = trunk(
        x_prompt, zeros(state_rwkv), zeros(state_shift), zeros(state_gla), zeros(state_conv), P)
    y_sample, rwkv_s, shift_s, gla_s, conv_s = trunk(
        x_sample, state_rwkv, state_shift, state_gla, state_conv, P)
    return (y_prompt, y_sample, rwkv_p, rwkv_s, shift_p, shift_s, gla_p, gla_s, conv_p, conv_s)
```

```python
import functools

import numpy as np
import jax
import jax.numpy as jnp
from jax import lax
from jax.experimental import pallas as pl
from jax.experimental.pallas import tpu as pltpu

D_MODEL = 1024
DEPTH = 4
A_WIDTH = 512
A_HEAD_DIM = 64
A_HEADS = 8
A_DECAY_LORA = 64
A_AAA_LORA = 64
A_MV_LORA = 32
A_GATE_LORA = 128
GN_EPS_A = 64e-5
B_WIDTH = 512
B_HEADS = 4
B_DV = 128
B_DK = 64
B_KW = 256
B_GATE_LORA = 16
GLA_LOGIT_NORM = 16.0
GLA_CHUNK = 64
RMS_EPS = 1e-5
D_FF = 2816
CONV_W = 3
ALPHA = (2 * DEPTH) ** 0.25
LN_EPS = 1e-5
SHIFT_W = 3 * A_WIDTH + A_DECAY_LORA + A_AAA_LORA + A_GATE_LORA
GLA_IN_W = 2 * B_KW + 2 * B_WIDTH + B_GATE_LORA
GLA_IN_PAD = 1664

LANES = 128
SUBLANES = 8
SCAN_GROUP = 8
VMEM_LIMIT = 56 * 1024 * 1024

_PERM = np.array([(c % A_HEADS) * A_HEAD_DIM + c // A_HEADS for c in range(A_WIDTH)], np.int32)
_COLS_A = np.concatenate([_PERM, A_WIDTH + _PERM, 2 * A_WIDTH + _PERM,
                          np.arange(3 * A_WIDTH, SHIFT_W, dtype=np.int32)])
_INV_COLS_A = np.argsort(_COLS_A).astype(np.int32)

_F32 = jnp.float32
_BF16 = jnp.bfloat16


def _cparams(*sem):
    return pltpu.CompilerParams(dimension_semantics=sem, vmem_limit_bytes=VMEM_LIMIT)


def _dot(a, b):
    return jnp.dot(a.astype(_BF16), b.astype(_BF16), preferred_element_type=_F32)


def _layer_norm(y, g, b):
    mu = jnp.mean(y, -1, keepdims=True)
    yc = y - mu
    var = jnp.mean(yc * yc, -1, keepdims=True)
    return yc * lax.rsqrt(var + LN_EPS) * g + b


def _proj_kernel(x_ref, wta_ref, wb_ref, pat_ref, pb_ref):
    x = x_ref[...].astype(_BF16)
    pat_ref[...] = lax.dot_general(wta_ref[...], x, (((1,), (1,)), ((), ())),
                                   preferred_element_type=_F32)
    pb_ref[...] = jnp.dot(x, wb_ref[...], preferred_element_type=_F32)


def _proj(x, wta, wb, tt):
    n = x.shape[0]
    return pl.pallas_call(
        _proj_kernel,
        grid=(n // tt,),
        in_specs=[pl.BlockSpec((tt, D_MODEL), lambda i: (i, 0)),
                  pl.BlockSpec((SHIFT_W, D_MODEL), lambda i: (0, 0)),
                  pl.BlockSpec((D_MODEL, GLA_IN_PAD), lambda i: (0, 0))],
        out_specs=[pl.BlockSpec((SHIFT_W, tt), lambda i: (0, i)),
                   pl.BlockSpec((tt, GLA_IN_PAD), lambda i: (i, 0))],
        out_shape=[jax.ShapeDtypeStruct((SHIFT_W, n), _F32),
                   jax.ShapeDtypeStruct((n, GLA_IN_PAD), _F32)],
        compiler_params=_cparams("parallel"),
        name="proj",
    )(x, wta, wb)


def _head_sum(x, tb):
    return jnp.sum(x.reshape(A_HEAD_DIM, A_HEADS, tb), axis=0)


def _head_bcast(s, tb):
    return jnp.broadcast_to(s[None], (A_HEAD_DIM, A_HEADS, tb)).reshape(A_WIDTH, tb)


def _prep_kernel(*refs, tb, period, has_vmix):
    it = iter(refs)
    pat_ref = next(it)
    bnd_ref = next(it) if period is not None else None
    vfirst_ref = next(it) if has_vmix else None
    mu_ref = next(it)
    prm_ref = next(it)
    wup_ref = next(it)
    aup_ref = next(it)
    gup_ref = next(it)
    v1_ref = next(it)
    v2_ref = next(it)
    ops_ref = next(it)
    g_ref = next(it)
    bonus_ref = next(it)
    carry_ref = next(it) if period is None else None

    pa = pat_ref[...]
    lane = lax.broadcasted_iota(jnp.int32, (SHIFT_W, tb), 1)
    rolled = pltpu.roll(pa, 1, 1)
    if period is None:
        @pl.when(pl.program_id(1) == 0)
        def _():
            carry_ref[...] = jnp.zeros_like(carry_ref)
        prev = jnp.where(lane == 0, pltpu.roll(carry_ref[...], 1, 1), rolled)
        carry_ref[...] = pa
    else:
        prev = jnp.where(lane % period == 0, bnd_ref[...], rolled)

    xs = pa + (prev - pa) * mu_ref[...]
    r = xs[0:A_WIDTH]
    k = xs[A_WIDTH:2 * A_WIDTH]
    v = xs[2 * A_WIDTH:3 * A_WIDTH]
    o0 = 3 * A_WIDTH
    w_lo = xs[o0:o0 + A_DECAY_LORA]
    a_lo = xs[o0 + A_DECAY_LORA:o0 + A_DECAY_LORA + A_AAA_LORA]
    g_lo = xs[o0 + A_DECAY_LORA + A_AAA_LORA:SHIFT_W]

    w0, a0, k_k, k_a, r_k = prm_ref[0], prm_ref[1], prm_ref[2], prm_ref[3], prm_ref[4]
    w = -jax.nn.softplus(-(w0 + _dot(wup_ref[...], jnp.tanh(w_lo)))) - 0.5
    decay = jnp.exp(-jnp.exp(w))
    a = jax.nn.sigmoid(a0 + _dot(aup_ref[...], a_lo))
    g = _dot(gup_ref[...], jax.nn.sigmoid(g_lo))
    if has_vmix:
        mix = jax.nn.sigmoid(prm_ref[5] + _dot(v2_ref[...], _dot(v1_ref[...], v)))
        v = v + (vfirst_ref[0] - v) * mix
    kk = k * k_k
    ss = _head_sum(kk * kk, tb)
    kk = kk * _head_bcast(lax.rsqrt(jnp.maximum(ss, 1e-24)), tb)
    kx = k * (1.0 + (a - 1.0) * k_a)
    ops_ref[0] = r
    ops_ref[1] = kk
    ops_ref[2] = decay
    ops_ref[3] = kk * a
    ops_ref[4] = kx
    ops_ref[5] = v
    g_ref[...] = g
    bonus_ref[...] = _head_bcast(_head_sum(r * kx * r_k, tb), tb) * v


def _prep(pat, bnd, vfirst_ops, mu, prm, wup, aup, gup, v1, v2, *, n_seq, seq_len, tb):
    n = pat.shape[1]
    prompt = bnd is None
    has_vmix = vfirst_ops is not None
    if prompt:
        nt = seq_len // tb
        grid = (n_seq, nt)
        tok = lambda b, t: b * nt + t
        sem = ("parallel", "arbitrary")
    else:
        grid = (n // tb,)
        tok = lambda i: i
        sem = ("parallel",)
    cmap = lambda f: (lambda *g: f(tok(*g)))
    in_specs = [pl.BlockSpec((SHIFT_W, tb), cmap(lambda j: (0, j)))]
    args = [pat]
    if not prompt:
        in_specs.append(pl.BlockSpec((SHIFT_W, tb), cmap(lambda j: (0, j))))
        args.append(bnd)
    if has_vmix:
        in_specs.append(pl.BlockSpec((1, A_WIDTH, tb), cmap(lambda j: (5, 0, j))))
        args.append(vfirst_ops)
    const2 = lambda *g: (0, 0)
    const3 = lambda *g: (0, 0, 0)
    in_specs += [pl.BlockSpec((SHIFT_W, tb), const2),
                 pl.BlockSpec((6, A_WIDTH, tb), const3),
                 pl.BlockSpec((A_WIDTH, A_DECAY_LORA), const2),
                 pl.BlockSpec((A_WIDTH, A_AAA_LORA), const2),
                 pl.BlockSpec((A_WIDTH, A_GATE_LORA), const2),
                 pl.BlockSpec((A_MV_LORA, A_WIDTH), const2),
                 pl.BlockSpec((A_WIDTH, A_MV_LORA), const2)]
    args += [mu, prm, wup, aup, gup, v1, v2]
    out_specs = [pl.BlockSpec((6, A_WIDTH, tb), cmap(lambda j: (0, 0, j))),
                 pl.BlockSpec((A_WIDTH, tb), cmap(lambda j: (0, j))),
                 pl.BlockSpec((A_WIDTH, tb), cmap(lambda j: (0, j)))]
    out_shape = [jax.ShapeDtypeStruct((6, A_WIDTH, n), _F32),
                 jax.ShapeDtypeStruct((A_WIDTH, n), _F32),
                 jax.ShapeDtypeStruct((A_WIDTH, n), _F32)]
    scratch = [pltpu.VMEM((SHIFT_W, tb), _F32)] if prompt else []
    return pl.pallas_call(
        functools.partial(_prep_kernel, tb=tb, period=None if prompt else seq_len,
                          has_vmix=has_vmix),
        grid=grid, in_specs=in_specs, out_specs=out_specs, out_shape=out_shape,
        scratch_shapes=scratch, compiler_params=_cparams(*sem), name="rwkv_prep",
    )(*args)


_VH = A_HEAD_DIM // 2


def _scan_kernel(*refs, tb, has_s0):
    it = iter(refs)
    zb_ref = next(it)
    zv_ref = next(it)
    s0_ref = next(it) if has_s0 else None
    o_ref = next(it)
    sout_ref = next(it)
    s_scr = next(it)

    t_blk = pl.program_id(1)

    @pl.when(t_blk == 0)
    def _():
        if has_s0:
            s_scr[...] = s0_ref[0]
        else:
            s_scr[...] = jnp.zeros_like(s_scr)

    def row(t, op, k):
        return zb_ref[0, t, pl.ds(op * A_HEAD_DIM + k, 1), :]

    def step(t, carry):
        acc = [jnp.zeros((_VH, LANES), _F32), jnp.zeros((_VH, LANES), _F32)]
        for k in range(A_HEAD_DIM):
            acc[k % 2] = acc[k % 2] + s_scr[k] * row(t, 1, k)
        sa = -(acc[0] + acc[1])
        vv = zv_ref[0, t]
        out = [jnp.zeros((_VH, LANES), _F32), jnp.zeros((_VH, LANES), _F32)]
        for k in range(A_HEAD_DIM):
            s_new = s_scr[k] * row(t, 2, k) + sa * row(t, 3, k) + vv * row(t, 4, k)
            s_scr[k] = s_new
            out[k % 2] = out[k % 2] + s_new * row(t, 0, k)
        o_ref[0, t] = out[0] + out[1]
        return carry

    lax.fori_loop(0, tb, step, 0)

    @pl.when(t_blk == pl.num_programs(1) - 1)
    def _():
        sout_ref[0] = s_scr[...]


def _scan(zb, zv, s0, tb):
    n_grp, seq_len = zb.shape[0], zb.shape[1]
    has_s0 = s0 is not None
    in_specs = [pl.BlockSpec((1, tb, 5 * A_HEAD_DIM, LANES), lambda g, t: (g, t, 0, 0)),
                pl.BlockSpec((1, tb, _VH, LANES), lambda g, t: (g, t, 0, 0))]
    args = [zb, zv]
    if has_s0:
        in_specs.append(pl.BlockSpec((1, A_HEAD_DIM, _VH, LANES), lambda g, t: (g, 0, 0, 0)))
        args.append(s0)
    return pl.pallas_call(
        functools.partial(_scan_kernel, tb=tb, has_s0=has_s0),
        grid=(n_grp, seq_len // tb),
        in_specs=in_specs,
        out_specs=[pl.BlockSpec((1, tb, _VH, LANES), lambda g, t: (g, t, 0, 0)),
                   pl.BlockSpec((1, A_HEAD_DIM, _VH, LANES), lambda g, t: (g, 0, 0, 0))],
        out_shape=[jax.ShapeDtypeStruct((n_grp, seq_len, _VH, LANES), _F32),
                   jax.ShapeDtypeStruct((n_grp, A_HEAD_DIM, _VH, LANES), _F32)],
        scratch_shapes=[pltpu.VMEM((A_HEAD_DIM, _VH, LANES), _F32)],
        compiler_params=_cparams("parallel", "arbitrary"), name="rwkv_scan",
    )(*args)


def _post_kernel(o_ref, g_ref, bonus_ref, lng_ref, lnb_ref, out_ref, *, tb):
    o3 = o_ref[...].reshape(A_HEAD_DIM, A_HEADS, tb)
    m = jnp.mean(o3, axis=0, keepdims=True)
    oc = o3 - m
    var = jnp.mean(oc * oc, axis=0, keepdims=True)
    on = (oc * lax.rsqrt(var + GN_EPS_A)).reshape(A_WIDTH, tb)
    y = (on * lng_ref[...] + lnb_ref[...] + bonus_ref[...]) * g_ref[...]
    out_ref[...] = y.T


def _post(ot, g, bonus, lng, lnb, tb):
    n = ot.shape[1]
    cm = pl.BlockSpec((A_WIDTH, tb), lambda i: (0, i))
    const = pl.BlockSpec((A_WIDTH, tb), lambda i: (0, 0))
    return pl.pallas_call(
        functools.partial(_post_kernel, tb=tb),
        grid=(n // tb,),
        in_specs=[cm, cm, cm, const, const],
        out_specs=pl.BlockSpec((tb, A_WIDTH), lambda i: (i, 0)),
        out_shape=jax.ShapeDtypeStruct((n, A_WIDTH), _F32),
        compiler_params=_cparams("parallel"), name="rwkv_post",
    )(ot, g, bonus, lng, lnb)


def _gla_kernel(*refs, nb, chunk, valid, has_s0):
    it = iter(refs)
    p_ref = next(it)
    s0_ref = next(it) if has_s0 else None
    fup_ref = next(it)
    fb_ref = next(it)
    ng_ref = next(it)
    o_ref = next(it)
    sout_ref = next(it)
    s_scr = next(it)

    c_idx = pl.program_id(1)

    @pl.when(c_idx == 0)
    def _():
        if has_s0:
            s_scr[...] = s0_ref[...]
        else:
            s_scr[...] = jnp.zeros_like(s_scr)

    row = lax.broadcasted_iota(jnp.int32, (chunk, chunk), 0)
    col = lax.broadcasted_iota(jnp.int32, (chunk, chunk), 1)
    causal = row >= col
    eye_k = (lax.broadcasted_iota(jnp.int32, (B_DK, B_DK), 0)
             == lax.broadcasted_iota(jnp.int32, (B_DK, B_DK), 1))
    trow = lax.broadcasted_iota(jnp.int32, (chunk, B_KW), 0)
    ng = ng_ref[...]

    for b in range(nb):
        q_all = p_ref[b, :, 0:B_KW]
        k_all = p_ref[b, :, B_KW:2 * B_KW]
        f_lo = p_ref[b, :, 2 * B_KW + 2 * B_WIDTH:GLA_IN_W]
        gk = jax.nn.log_sigmoid(_dot(f_lo, fup_ref[...]) + fb_ref[...]) / GLA_LOGIT_NORM
        if valid < chunk:
            gk = jnp.where(trow < valid, gk, 0.0)
            k_all = jnp.where(trow < valid, k_all, 0.0)
        bc = gk
        s = 1
        while s < chunk:
            bc = bc + jnp.where(trow >= s, pltpu.roll(bc, s, 0), 0.0)
            s *= 2
        e_pos = jnp.exp(bc)
        e_neg = jnp.exp(-bc)
        b_last = bc[chunk - 1:chunk, :]
        e_rem = jnp.exp(b_last - bc)
        e_last = jnp.exp(b_last)
        qt_all = q_all * (B_DK ** -0.5) * e_pos
        kt_all = k_all * e_neg
        kd_all = k_all * e_rem
        for h in range(B_HEADS):
            ks = slice(h * B_DK, (h + 1) * B_DK)
            vs = slice(2 * B_KW + h * B_DV, 2 * B_KW + (h + 1) * B_DV)
            gs = slice(2 * B_KW + B_WIDTH + h * B_DV, 2 * B_KW + B_WIDTH + (h + 1) * B_DV)
            qt = qt_all[:, ks].astype(_BF16)
            kt = kt_all[:, ks].astype(_BF16)
            kd = kd_all[:, ks].astype(_BF16)
            vh = p_ref[b, :, vs]
            vb = vh.astype(_BF16)
            gh = p_ref[b, :, gs]
            st = s_scr[b, h]
            att = lax.dot_general(qt, kt, (((1,), (1,)), ((), ())), preferred_element_type=_F32)
            att = jnp.where(causal, att, 0.0)
            o = (jnp.dot(att.astype(_BF16), vb, preferred_element_type=_F32)
                 + jnp.dot(qt, st.astype(_BF16), preferred_element_type=_F32))
            e_col = jnp.sum(jnp.where(eye_k, jnp.broadcast_to(e_last[:, ks], (B_DK, B_DK)), 0.0),
                            axis=1, keepdims=True)
            s_scr[b, h] = st * e_col + lax.dot_general(
                kd, vb, (((0,), (0,)), ((), ())), preferred_element_type=_F32)
            on = o * lax.rsqrt(jnp.mean(o * o, -1, keepdims=True) + RMS_EPS) * ng
            o_ref[b, :, h * B_DV:(h + 1) * B_DV] = on * (gh * jax.nn.sigmoid(gh))

    @pl.when(c_idx == pl.num_programs(1) - 1)
    def _():
        sout_ref[...] = s_scr[...]


def _gla(pb3, s0, fup, fb, ng, *, nb, chunk, valid):
    n_seq, seq_len = pb3.shape[0], pb3.shape[1]
    has_s0 = s0 is not None
    in_specs = [pl.BlockSpec((nb, chunk, GLA_IN_PAD), lambda i, c: (i, c, 0))]
    args = [pb3]
    if has_s0:
        in_specs.append(pl.BlockSpec((nb, B_HEADS, B_DK, B_DV), lambda i, c: (i, 0, 0, 0)))
        args.append(s0)
    in_specs += [pl.BlockSpec((B_GATE_LORA, B_KW), lambda i, c: (0, 0)),
                 pl.BlockSpec((1, B_KW), lambda i, c: (0, 0)),
                 pl.BlockSpec((1, B_DV), lambda i, c: (0, 0))]
    args += [fup, fb, ng]
    return pl.pallas_call(
        functools.partial(_gla_kernel, nb=nb, chunk=chunk, valid=valid, has_s0=has_s0),
        grid=(n_seq // nb, seq_len // chunk),
        in_specs=in_specs,
        out_specs=[pl.BlockSpec((nb, chunk, B_WIDTH), lambda i, c: (i, c, 0)),
                   pl.BlockSpec((nb, B_HEADS, B_DK, B_DV), lambda i, c: (i, 0, 0, 0))],
        out_shape=[jax.ShapeDtypeStruct((n_seq, seq_len, B_WIDTH), _F32),
                   jax.ShapeDtypeStruct((n_seq, B_HEADS, B_DK, B_DV), _F32)],
        scratch_shapes=[pltpu.VMEM((nb, B_HEADS, B_DK, B_DV), _F32)],
        compiler_params=_cparams("parallel", "arbitrary"), name="gla",
    )(*args)


def _outproj_kernel(x_ref, oa_ref, ob_ref, wa_ref, wb_ref, g_ref, b_ref, y_ref):
    mix = _dot(oa_ref[...], wa_ref[...]) + _dot(ob_ref[...], wb_ref[...])
    y_ref[...] = _layer_norm(ALPHA * x_ref[...] + mix, g_ref[...], b_ref[...])


def _outproj(x, oa, ob, wa, wb, g, b, tm):
    n = x.shape[0]
    rowb = lambda w: pl.BlockSpec((tm, w), lambda i: (i, 0))
    const = lambda r, c: pl.BlockSpec((r, c), lambda i: (0, 0))
    return pl.pallas_call(
        _outproj_kernel,
        grid=(n // tm,),
        in_specs=[rowb(D_MODEL), rowb(A_WIDTH), rowb(B_WIDTH), const(A_WIDTH, D_MODEL),
                  const(B_WIDTH, D_MODEL), const(1, D_MODEL), const(1, D_MODEL)],
        out_specs=rowb(D_MODEL),
        out_shape=jax.ShapeDtypeStruct((n, D_MODEL), _F32),
        compiler_params=_cparams("parallel"), name="outproj_ln",
    )(x, oa, ob, wa, wb, g, b)


def _ffn_kernel(*refs, tm, period):
    it = iter(refs)
    x_ref = next(it)
    p1_ref = next(it) if period is not None else None
    p2_ref = next(it) if period is not None else None
    wup_ref = next(it)
    cw_ref = next(it)
    cb_ref = next(it)
    wdn_ref = next(it)
    g_ref = next(it)
    b_ref = next(it)
    y_ref = next(it)
    gate_ref = next(it)
    carry_ref = next(it) if period is None else None

    x = x_ref[...]
    u = jnp.dot(x.astype(_BF16), wup_ref[...], preferred_element_type=_F32)
    gate = u[:, :D_FF]
    val = u[:, D_FF:]
    row = lax.broadcasted_iota(jnp.int32, (tm, D_FF), 0)
    r1 = pltpu.roll(gate, 1, 0)
    r2 = pltpu.roll(gate, 2, 0)
    if period is None:
        @pl.when(pl.program_id(1) == 0)
        def _():
            carry_ref[...] = jnp.zeros_like(carry_ref)
        c = carry_ref[...]
        row8 = lax.broadcasted_iota(jnp.int32, (SUBLANES, D_FF), 0)
        top1 = jnp.where(row8 < 1, pltpu.roll(c, 1, 0), r1[:SUBLANES])
        top2 = jnp.where(row8 < 2, pltpu.roll(c, 2, 0), r2[:SUBLANES])
        g1 = jnp.concatenate([top1, r1[SUBLANES:]], axis=0)
        g2 = jnp.concatenate([top2, r2[SUBLANES:]], axis=0)
        carry_ref[...] = gate[tm - SUBLANES:]
        gate_ref[0] = gate[tm - SUBLANES:]
    else:
        g1 = jnp.where(row % period >= 1, r1, p1_ref[...])
        g2 = jnp.where(row % period >= 2, r2, p2_ref[...])
        gate_ref[...] = gate
    acc = cb_ref[...] + g2 * cw_ref[0:1, :] + g1 * cw_ref[1:2, :] + gate * cw_ref[2:3, :]
    h = 0.5 * acc * (1.0 + lax.erf(acc * (2.0 ** -0.5))) * val
    f = jnp.dot(h.astype(_BF16), wdn_ref[...], preferred_element_type=_F32)
    y_ref[...] = _layer_norm(ALPHA * x + f, g_ref[...], b_ref[...])


def _ffn(x, p1, p2, wup, cw, cb, wdn, g, b, *, n_seq, seq_len, tm):
    n = x.shape[0]
    prompt = p1 is None
    if prompt:
        nt = seq_len // tm
        grid = (n_seq, nt)
        rmap = lambda bb, t: (bb * nt + t, 0)
        sem = ("parallel", "arbitrary")
    else:
        grid = (n // tm,)
        rmap = lambda i: (i, 0)
        sem = ("parallel",)
    const = lambda r, c: pl.BlockSpec((r, c), lambda *gidx: (0, 0), pipeline_mode=pl.Buffered(1))
    in_specs = [pl.BlockSpec((tm, D_MODEL), rmap)]
    args = [x]
    if not prompt:
        in_specs += [pl.BlockSpec((tm, D_FF), rmap), pl.BlockSpec((tm, D_FF), rmap)]
        args += [p1, p2]
    in_specs += [const(D_MODEL, 2 * D_FF), const(CONV_W, D_FF), const(1, D_FF),
                 const(D_FF, D_MODEL), const(1, D_MODEL), const(1, D_MODEL)]
    args += [wup, cw, cb, wdn, g, b]
    if prompt:
        gate_spec = pl.BlockSpec((1, SUBLANES, D_FF), lambda bb, t: (bb, 0, 0))
        gate_shape = jax.ShapeDtypeStruct((n_seq, SUBLANES, D_FF), _F32)
        scratch = [pltpu.VMEM((SUBLANES, D_FF), _F32)]
    else:
        gate_spec = pl.BlockSpec((tm, D_FF), rmap)
        gate_shape = jax.ShapeDtypeStruct((n, D_FF), _F32)
        scratch = []
    return pl.pallas_call(
        functools.partial(_ffn_kernel, tm=tm, period=None if prompt else seq_len),
        grid=grid, in_specs=in_specs,
        out_specs=[pl.BlockSpec((tm, D_MODEL), rmap), gate_spec],
        out_shape=[jax.ShapeDtypeStruct((n, D_MODEL), _F32), gate_shape],
        scratch_shapes=scratch, compiler_params=_cparams(*sem), name="ffn_ln",
    )(*args)


def _ops_to_scan(ops, n_grp, seq_len):
    x = ops.reshape(6, A_HEAD_DIM, A_HEADS, n_grp, SCAN_GROUP, seq_len)
    zb = jnp.transpose(x[:5], (3, 5, 0, 1, 4, 2)).reshape(n_grp, seq_len, 5 * A_HEAD_DIM, 64)
    zb = jnp.concatenate([zb, zb], axis=-1)
    xv = x[5].reshape(2, _VH, A_HEADS, n_grp, SCAN_GROUP, seq_len)
    zv = jnp.transpose(xv, (3, 5, 1, 0, 4, 2)).reshape(n_grp, seq_len, _VH, LANES)
    return zb, zv


def _scan_to_cm(o, n_grp, seq_len):
    x = o.reshape(n_grp, seq_len, _VH, 2, SCAN_GROUP, A_HEADS)
    return jnp.transpose(x, (3, 2, 5, 0, 4, 1)).reshape(A_WIDTH, n_grp * SCAN_GROUP * seq_len)


def _state_to_scan(s, n_grp):
    x = s.reshape(n_grp, SCAN_GROUP, A_HEADS, 2, _VH, A_HEAD_DIM)
    return jnp.transpose(x, (0, 5, 4, 3, 1, 2)).reshape(n_grp, A_HEAD_DIM, _VH, LANES)


def _scan_to_state(s, n_grp):
    x = s.reshape(n_grp, A_HEAD_DIM, _VH, 2, SCAN_GROUP, A_HEADS)
    return jnp.transpose(x, (0, 4, 5, 3, 2, 1)).reshape(
        n_grp * SCAN_GROUP, A_HEADS, A_HEAD_DIM, A_HEAD_DIM)


def _bcast_cols(p, tb):
    return jnp.broadcast_to(p[:, None], (p.shape[0], tb))


def _layer_weights(P, l, tb_prep, tb_post):
    w_in = P['w_in'][l]
    wta = w_in[:, _COLS_A].T.astype(_BF16)
    wb = jnp.pad(w_in[:, SHIFT_W:], ((0, 0), (0, GLA_IN_PAD - GLA_IN_W))).astype(_BF16)
    perm = _PERM
    vb = P['vres_bias'][l - 1][perm] if l > 0 else jnp.zeros((A_WIDTH,), _F32)
    prm = jnp.stack([P['w0'][l][perm], P['a0'][l][perm], P['k_k'][l][perm], P['k_a'][l][perm],
                     P['r_k'][l].reshape(A_WIDTH)[perm], vb])
    prm = jnp.broadcast_to(prm[:, :, None], (6, A_WIDTH, tb_prep))
    if l > 0:
        v1 = P['vres_down'][l - 1][perm].T.astype(_BF16)
        v2 = P['vres_up'][l - 1][:, perm].T.astype(_BF16)
    else:
        v1 = jnp.zeros((A_MV_LORA, A_WIDTH), _BF16)
        v2 = jnp.zeros((A_WIDTH, A_MV_LORA), _BF16)
    return dict(
        wta=wta, wb=wb,
        mu=_bcast_cols(P['tok_mu'][l][_COLS_A], tb_prep), prm=prm,
        wup=P['w_lora_up'][l][:, perm].T.astype(_BF16),
        aup=P['a_lora_up'][l][:, perm].T.astype(_BF16),
        gup=P['g_lora_up'][l][:, perm].T.astype(_BF16),
        v1=v1, v2=v2,
        lng=_bcast_cols(P['lnx_g'][l][perm], tb_post), lnb=_bcast_cols(P['lnx_b'][l][perm], tb_post),
        fup=P['gla_f_up'][l].astype(_BF16), fb=P['gla_f_bias'][l][None], ng=P['gla_norm_g'][l][None],
        woa=P['w_out'][l][:A_WIDTH][perm].astype(_BF16), wob=P['w_out'][l][A_WIDTH:].astype(_BF16),
        ln1g=P['ln1_g'][l][None], ln1b=P['ln1_b'][l][None],
        w_up=P['w_up'][l].astype(_BF16), cw=P['conv_w'][l], cb=P['conv_b'][l][None],
        w_down=P['w_down'][l].astype(_BF16), ln2g=P['ln2_g'][l][None], ln2b=P['ln2_b'][l][None])


def _trunk(x3, states, P, prompt):
    n_seq, seq_len, _ = x3.shape
    n = n_seq * seq_len
    n_grp = n_seq // SCAN_GROUP
    x = x3.reshape(n, D_MODEL)
    tb = LANES
    if prompt:
        tt, scan_tb, gla_chunk, gla_valid, gla_t, tm_out, tm_ffn = 512, 32, GLA_CHUNK, GLA_CHUNK, seq_len, 256, 256
    else:
        st_rwkv, st_shift, st_gla, st_conv = states
        tt, scan_tb, gla_chunk, gla_valid, gla_t, tm_out, tm_ffn = 512, seq_len, SUBLANES, seq_len, SUBLANES, 256, 256
    new_rwkv, new_shift, new_gla, new_conv = [], [], [], []
    ops0 = None
    for l in range(DEPTH):
        W = _layer_weights(P, l, tb, tb)
        pat, pb = _proj(x, W['wta'], W['wb'], tt)
        if prompt:
            bnd = None
        else:
            bnd = jnp.repeat(st_shift[l][:, 0, _COLS_A].T, seq_len, axis=1)
        ops, g, bonus = _prep(pat, bnd, ops0, W['mu'], W['prm'], W['wup'], W['aup'], W['gup'],
                              W['v1'], W['v2'], n_seq=n_seq, seq_len=seq_len, tb=tb)
        if l == 0:
            ops0 = ops
        zb, zv = _ops_to_scan(ops, n_grp, seq_len)
        s0 = None if prompt else _state_to_scan(st_rwkv[l], n_grp)
        o, s_fin = _scan(zb, zv, s0, scan_tb)
        oa = _post(_scan_to_cm(o, n_grp, seq_len), g, bonus, W['lng'], W['lnb'], tb)
        pb3 = pb.reshape(n_seq, seq_len, GLA_IN_PAD)
        if gla_t != seq_len:
            pb3 = jnp.pad(pb3, ((0, 0), (0, gla_t - seq_len), (0, 0)))
        ob, s_gla = _gla(pb3, None if prompt else st_gla[l], W['fup'], W['fb'], W['ng'],
                         nb=SCAN_GROUP, chunk=gla_chunk, valid=gla_valid)
        ob = ob[:, :seq_len].reshape(n, B_WIDTH)
        x1 = _outproj(x, oa, ob, W['woa'], W['wob'], W['ln1g'], W['ln1b'], tm_out)
        if prompt:
            p1 = p2 = None
        else:
            past = st_conv[l]
            zero = jnp.zeros((n_seq, 1, D_FF), _F32)
            p1 = jnp.concatenate([past[:, 1:2], zero, zero, zero], axis=1).reshape(n, D_FF)
            p2 = jnp.concatenate([past[:, 0:1], past[:, 1:2], zero, zero], axis=1).reshape(n, D_FF)
        x, gate = _ffn(x1, p1, p2, W['w_up'], W['cw'], W['cb'], W['w_down'], W['ln2g'], W['ln2b'],
                       n_seq=n_seq, seq_len=seq_len, tm=tm_ffn)
        new_rwkv.append(_scan_to_state(s_fin, n_grp))
        new_shift.append(pat[:, seq_len - 1::seq_len].T[:, _INV_COLS_A][:, None, :])
        new_gla.append(s_gla)
        if prompt:
            new_conv.append(gate[:, SUBLANES - (CONV_W - 1):])
        else:
            new_conv.append(gate.reshape(n_seq, seq_len, D_FF)[:, seq_len - (CONV_W - 1):])
    return (x.reshape(n_seq, seq_len, D_MODEL), jnp.stack(new_rwkv), jnp.stack(new_shift),
            jnp.stack(new_gla), jnp.stack(new_conv))


def kernel(x_prompt, x_sample, state_rwkv, state_shift, state_gla, state_conv, w_in, tok_mu, w0, w_lora_up, a0, a_lora_up, g_lora_up, k_k, k_a, r_k, lnx_g, lnx_b, vres_bias, vres_down, vres_up, gla_f_up, gla_f_bias, gla_norm_g, w_out, ln1_g, ln1_b, w_up, conv_w, conv_b, w_down, ln2_g, ln2_b):
    P = dict(w_in=w_in, tok_mu=tok_mu, w0=w0, w_lora_up=w_lora_up, a0=a0, a_lora_up=a_lora_up,
             g_lora_up=g_lora_up, k_k=k_k, k_a=k_a, r_k=r_k, lnx_g=lnx_g, lnx_b=lnx_b,
             vres_bias=vres_bias, vres_down=vres_down, vres_up=vres_up, gla_f_up=gla_f_up,
             gla_f_bias=gla_f_bias, gla_norm_g=gla_norm_g, w_out=w_out, ln1_g=ln1_g, ln1_b=ln1_b,
             w_up=w_up, conv_w=conv_w, conv_b=conv_b, w_down=w_down, ln2_g=ln2_g, ln2_b=ln2_b)
    y_p, rwkv_p, shift_p, gla_p, conv_p = _trunk(x_prompt, None, P, True)
    y_s, rwkv_s, shift_s, gla_s, conv_s = _trunk(
        x_sample, (state_rwkv, state_shift, state_gla, state_conv), P, False)
    return (y_p, y_s, rwkv_p, rwkv_s, shift_p, shift_s, gla_p, gla_s, conv_p, conv_s)
```

```python
import functools

import numpy as np
import jax
import jax.numpy as jnp
from jax import lax
from jax.experimental import pallas as pl
from jax.experimental.pallas import tpu as pltpu

D_MODEL = 1024
DEPTH = 4
A_WIDTH = 512
A_HEAD_DIM = 64
A_HEADS = 8
A_DECAY_LORA = 64
A_AAA_LORA = 64
A_MV_LORA = 32
A_GATE_LORA = 128
GN_EPS_A = 64e-5
B_WIDTH = 512
B_HEADS = 4
B_DV = 128
B_DK = 64
B_KW = 256
B_GATE_LORA = 16
GLA_LOGIT_NORM = 16.0
GLA_CHUNK = 64
RMS_EPS = 1e-5
D_FF = 2816
CONV_W = 3
ALPHA = (2 * DEPTH) ** 0.25
LN_EPS = 1e-5
SHIFT_W = 3 * A_WIDTH + A_DECAY_LORA + A_AAA_LORA + A_GATE_LORA
GLA_IN_W = 2 * B_KW + 2 * B_WIDTH + B_GATE_LORA
GLA_IN_PAD = 1664

LANES = 128
SUBLANES = 8
SCAN_GROUP = 8
VMEM_LIMIT = 56 * 1024 * 1024

_PERM = np.array([(c % A_HEADS) * A_HEAD_DIM + c // A_HEADS for c in range(A_WIDTH)], np.int32)
_COLS_A = np.concatenate([_PERM, A_WIDTH + _PERM, 2 * A_WIDTH + _PERM,
                          np.arange(3 * A_WIDTH, SHIFT_W, dtype=np.int32)])
_INV_COLS_A = np.argsort(_COLS_A).astype(np.int32)

_F32 = jnp.float32
_BF16 = jnp.bfloat16


def _cparams(*sem):
    return pltpu.CompilerParams(dimension_semantics=sem, vmem_limit_bytes=VMEM_LIMIT)


def _dot(a, b):
    return jnp.dot(a.astype(_BF16), b.astype(_BF16), preferred_element_type=_F32)


def _layer_norm(y, g, b):
    mu = jnp.mean(y, -1, keepdims=True)
    yc = y - mu
    var = jnp.mean(yc * yc, -1, keepdims=True)
    return yc * lax.rsqrt(var + LN_EPS) * g + b


def _proj_kernel(x_ref, wta_ref, wb_ref, pat_ref, pb_ref):
    x = x_ref[...].astype(_BF16)
    pat_ref[...] = lax.dot_general(wta_ref[...], x, (((1,), (1,)), ((), ())),
                                   preferred_element_type=_F32)
    pb_ref[...] = jnp.dot(x, wb_ref[...], preferred_element_type=_F32)


def _proj(x, wta, wb, tt):
    n = x.shape[0]
    return pl.pallas_call(
        _proj_kernel,
        grid=(n // tt,),
        in_specs=[pl.BlockSpec((tt, D_MODEL), lambda i: (i, 0)),
                  pl.BlockSpec((SHIFT_W, D_MODEL), lambda i: (0, 0)),
                  pl.BlockSpec((D_MODEL, GLA_IN_PAD), lambda i: (0, 0))],
        out_specs=[pl.BlockSpec((SHIFT_W, tt), lambda i: (0, i)),
                   pl.BlockSpec((tt, GLA_IN_PAD), lambda i: (i, 0))],
        out_shape=[jax.ShapeDtypeStruct((SHIFT_W, n), _F32),
                   jax.ShapeDtypeStruct((n, GLA_IN_PAD), _F32)],
        compiler_params=_cparams("parallel"),
        name="proj",
    )(x, wta, wb)


def _head_sum(x, tb):
    return jnp.sum(x.reshape(A_HEAD_DIM, A_HEADS, tb), axis=0)


def _head_bcast(s, tb):
    return jnp.broadcast_to(s[None], (A_HEAD_DIM, A_HEADS, tb)).reshape(A_WIDTH, tb)


def _prep_kernel(*refs, tb, period, has_vmix):
    it = iter(refs)
    pat_ref = next(it)
    bnd_ref = next(it) if period is not None else None
    vfirst_ref = next(it) if has_vmix else None
    mu_ref = next(it)
    prm_ref = next(it)
    wup_ref = next(it)
    aup_ref = next(it)
    gup_ref = next(it)
    v1_ref = next(it)
    v2_ref = next(it)
    ops_ref = next(it)
    g_ref = next(it)
    bonus_ref = next(it)
    last_ref = next(it) if period is None else None
    carry_ref = next(it) if period is None else None

    pa = pat_ref[...]
    lane = lax.broadcasted_iota(jnp.int32, (SHIFT_W, tb), 1)
    rolled = pltpu.roll(pa, 1, 1)
    if period is None:
        @pl.when(pl.program_id(1) == 0)
        def _():
            carry_ref[...] = jnp.zeros_like(carry_ref)
        prev = jnp.where(lane == 0, pltpu.roll(carry_ref[...], 1, 1), rolled)
        carry_ref[...] = pa
        last_ref[0] = pa
    else:
        prev = jnp.where(lane % period == 0, bnd_ref[...], rolled)

    xs = pa + (prev - pa) * mu_ref[...]
    r = xs[0:A_WIDTH]
    k = xs[A_WIDTH:2 * A_WIDTH]
    v = xs[2 * A_WIDTH:3 * A_WIDTH]
    o0 = 3 * A_WIDTH
    w_lo = xs[o0:o0 + A_DECAY_LORA]
    a_lo = xs[o0 + A_DECAY_LORA:o0 + A_DECAY_LORA + A_AAA_LORA]
    g_lo = xs[o0 + A_DECAY_LORA + A_AAA_LORA:SHIFT_W]

    w0, a0, k_k, k_a, r_k = prm_ref[0], prm_ref[1], prm_ref[2], prm_ref[3], prm_ref[4]
    w = -jax.nn.softplus(-(w0 + _dot(wup_ref[...], jnp.tanh(w_lo)))) - 0.5
    decay = jnp.exp(-jnp.exp(w))
    a = jax.nn.sigmoid(a0 + _dot(aup_ref[...], a_lo))
    g = _dot(gup_ref[...], jax.nn.sigmoid(g_lo))
    if has_vmix:
        mix = jax.nn.sigmoid(prm_ref[5] + _dot(v2_ref[...], _dot(v1_ref[...], v)))
        v = v + (vfirst_ref[0] - v) * mix
    kk = k * k_k
    ss = _head_sum(kk * kk, tb)
    kk = kk * _head_bcast(lax.rsqrt(jnp.maximum(ss, 1e-24)), tb)
    kx = k * (1.0 + (a - 1.0) * k_a)
    ops_ref[0] = r
    ops_ref[1] = kk
    ops_ref[2] = decay
    ops_ref[3] = kk * a
    ops_ref[4] = kx
    ops_ref[5] = v
    g_ref[...] = g
    bonus_ref[...] = _head_bcast(_head_sum(r * kx * r_k, tb), tb) * v


def _prep(pat, bnd, vfirst_ops, mu, prm, wup, aup, gup, v1, v2, *, n_seq, seq_len, tb):
    n = pat.shape[1]
    prompt = bnd is None
    has_vmix = vfirst_ops is not None
    if prompt:
        nt = seq_len // tb
        grid = (n_seq, nt)
        tok = lambda b, t: b * nt + t
        sem = ("parallel", "arbitrary")
    else:
        grid = (n // tb,)
        tok = lambda i: i
        sem = ("parallel",)
    cmap = lambda f: (lambda *g: f(tok(*g)))
    in_specs = [pl.BlockSpec((SHIFT_W, tb), cmap(lambda j: (0, j)))]
    args = [pat]
    if not prompt:
        in_specs.append(pl.BlockSpec((SHIFT_W, tb), cmap(lambda j: (0, j))))
        args.append(bnd)
    if has_vmix:
        in_specs.append(pl.BlockSpec((1, A_WIDTH, tb), cmap(lambda j: (5, 0, j))))
        args.append(vfirst_ops)
    const2 = lambda *g: (0, 0)
    const3 = lambda *g: (0, 0, 0)
    in_specs += [pl.BlockSpec((SHIFT_W, tb), const2),
                 pl.BlockSpec((6, A_WIDTH, tb), const3),
                 pl.BlockSpec((A_WIDTH, A_DECAY_LORA), const2),
                 pl.BlockSpec((A_WIDTH, A_AAA_LORA), const2),
                 pl.BlockSpec((A_WIDTH, A_GATE_LORA), const2),
                 pl.BlockSpec((A_MV_LORA, A_WIDTH), const2),
                 pl.BlockSpec((A_WIDTH, A_MV_LORA), const2)]
    args += [mu, prm, wup, aup, gup, v1, v2]
    out_specs = [pl.BlockSpec((6, A_WIDTH, tb), cmap(lambda j: (0, 0, j))),
                 pl.BlockSpec((A_WIDTH, tb), cmap(lambda j: (0, j))),
                 pl.BlockSpec((A_WIDTH, tb), cmap(lambda j: (0, j)))]
    out_shape = [jax.ShapeDtypeStruct((6, A_WIDTH, n), _F32),
                 jax.ShapeDtypeStruct((A_WIDTH, n), _F32),
                 jax.ShapeDtypeStruct((A_WIDTH, n), _F32)]
    scratch = [pltpu.VMEM((SHIFT_W, tb), _F32)] if prompt else []
    if prompt:
        out_specs.append(pl.BlockSpec((1, SHIFT_W, tb), lambda b, t: (b, 0, 0)))
        out_shape.append(jax.ShapeDtypeStruct((n_seq, SHIFT_W, tb), _F32))
    return pl.pallas_call(
        functools.partial(_prep_kernel, tb=tb, period=None if prompt else seq_len,
                          has_vmix=has_vmix),
        grid=grid, in_specs=in_specs, out_specs=out_specs, out_shape=out_shape,
        scratch_shapes=scratch, compiler_params=_cparams(*sem), name="rwkv_prep",
    )(*args)


_VH = A_HEAD_DIM // 2


_OP_R, _OP_KK, _OP_DECAY, _OP_B, _OP_K, _OP_V = range(6)


def _scan_steps(s_scr, row, load_v, store_o, n_steps):
    def step(t, carry):
        acc = [jnp.zeros((_VH, LANES), _F32), jnp.zeros((_VH, LANES), _F32)]
        for k in range(A_HEAD_DIM):
            acc[k % 2] = acc[k % 2] + s_scr[k] * row(t, _OP_KK, k)
        sa = -(acc[0] + acc[1])
        vv = load_v(t)
        out = [jnp.zeros((_VH, LANES), _F32), jnp.zeros((_VH, LANES), _F32)]
        for k in range(A_HEAD_DIM):
            s_new = (s_scr[k] * row(t, _OP_DECAY, k) + sa * row(t, _OP_B, k)
                     + vv * row(t, _OP_K, k))
            s_scr[k] = s_new
            out[k % 2] = out[k % 2] + s_new * row(t, _OP_R, k)
        store_o(t, out[0] + out[1])
        return carry

    lax.fori_loop(0, n_steps, step, 0)


def _scan_kernel(*refs, tb, has_s0):
    it = iter(refs)
    zb_ref = next(it)
    zv_ref = next(it)
    s0_ref = next(it) if has_s0 else None
    o_ref = next(it)
    sout_ref = next(it)
    s_scr = next(it)

    t_blk = pl.program_id(1)

    @pl.when(t_blk == 0)
    def _():
        if has_s0:
            s_scr[...] = s0_ref[0]
        else:
            s_scr[...] = jnp.zeros_like(s_scr)

    def row(t, op, k):
        return zb_ref[0, t, pl.ds(op * A_HEAD_DIM + k, 1), :]

    def store_o(t, val):
        o_ref[0, t] = val

    _scan_steps(s_scr, row, lambda t: zv_ref[0, t], store_o, tb)

    @pl.when(t_blk == pl.num_programs(1) - 1)
    def _():
        sout_ref[0] = s_scr[...]


def _scan_cm_kernel(*refs, tb):
    in_refs = refs[:SCAN_GROUP]
    o_ref, sout_ref, z_scr, zv_scr, o_scr, s_scr = refs[SCAN_GROUP:]
    t_blk = pl.program_id(0)
    phase = pl.program_id(1)

    @pl.when(jnp.logical_and(t_blk == 0, phase == 0))
    def _():
        s_scr[...] = jnp.zeros_like(s_scr)

    @pl.when(phase < _OP_V)
    def _():
        for k in range(A_HEAD_DIM):
            rows = slice(k * A_HEADS, (k + 1) * A_HEADS)
            m = jnp.concatenate([r[0, rows, :] for r in in_refs], axis=0)
            z_scr[phase * A_HEAD_DIM + k] = jnp.concatenate([m, m], axis=0).T

    @pl.when(phase == _OP_V)
    def _():
        for v in range(_VH):
            lo = slice(v * A_HEADS, (v + 1) * A_HEADS)
            hi = slice((_VH + v) * A_HEADS, (_VH + v + 1) * A_HEADS)
            m = jnp.concatenate([r[0, lo, :] for r in in_refs]
                                + [r[0, hi, :] for r in in_refs], axis=0)
            zv_scr[pl.ds(v * tb, tb), :] = m.T

    @pl.when(phase == _OP_V + 1)
    def _():
        def row(t, op, k):
            return z_scr[op * A_HEAD_DIM + k, pl.ds(t, 1), :]

        def load_v(t):
            return zv_scr[pl.ds(t, _VH, stride=tb), :]

        def store_o(t, val):
            o_scr[pl.ds(pl.multiple_of(t * _VH, _VH), _VH), :] = val

        _scan_steps(s_scr, row, load_v, store_o, tb)
        for v in range(_VH):
            mt = o_scr[pl.ds(v, tb, stride=_VH), :].T
            for half in range(2):
                for b in range(SCAN_GROUP):
                    src = (half * SCAN_GROUP + b) * A_HEADS
                    dst = (half * _VH + v) * A_HEADS
                    o_ref[b, dst:dst + A_HEADS, :] = mt[src:src + A_HEADS, :]
        sout_ref[...] = s_scr[...]


def _scan_cm(ops, seq_len, tb):
    nt = seq_len // tb
    n_phase = _OP_V + 2

    def in_spec(b):
        return pl.BlockSpec((1, A_WIDTH, tb),
                            lambda t, p: (jnp.minimum(p, _OP_V), 0, b * nt + t))

    return pl.pallas_call(
        functools.partial(_scan_cm_kernel, tb=tb),
        grid=(nt, n_phase),
        in_specs=[in_spec(b) for b in range(SCAN_GROUP)],
        out_specs=[pl.BlockSpec((SCAN_GROUP, A_WIDTH, tb), lambda t, p: (0, 0, t)),
                   pl.BlockSpec((A_HEAD_DIM, _VH, LANES), lambda t, p: (0, 0, 0))],
        out_shape=[jax.ShapeDtypeStruct((SCAN_GROUP, A_WIDTH, seq_len), _F32),
                   jax.ShapeDtypeStruct((A_HEAD_DIM, _VH, LANES), _F32)],
        scratch_shapes=[pltpu.VMEM((5 * A_HEAD_DIM, tb, LANES), _F32),
                        pltpu.VMEM((_VH * tb, LANES), _F32),
                        pltpu.VMEM((tb * _VH, LANES), _F32),
                        pltpu.VMEM((A_HEAD_DIM, _VH, LANES), _F32)],
        compiler_params=_cparams("arbitrary", "arbitrary"), name="rwkv_scan_cm",
    )(*([ops] * SCAN_GROUP))


def _scan(zb, zv, s0, tb):
    n_grp, seq_len = zb.shape[0], zb.shape[1]
    has_s0 = s0 is not None
    in_specs = [pl.BlockSpec((1, tb, 5 * A_HEAD_DIM, LANES), lambda g, t: (g, t, 0, 0)),
                pl.BlockSpec((1, tb, _VH, LANES), lambda g, t: (g, t, 0, 0))]
    args = [zb, zv]
    if has_s0:
        in_specs.append(pl.BlockSpec((1, A_HEAD_DIM, _VH, LANES), lambda g, t: (g, 0, 0, 0)))
        args.append(s0)
    return pl.pallas_call(
        functools.partial(_scan_kernel, tb=tb, has_s0=has_s0),
        grid=(n_grp, seq_len // tb),
        in_specs=in_specs,
        out_specs=[pl.BlockSpec((1, tb, _VH, LANES), lambda g, t: (g, t, 0, 0)),
                   pl.BlockSpec((1, A_HEAD_DIM, _VH, LANES), lambda g, t: (g, 0, 0, 0))],
        out_shape=[jax.ShapeDtypeStruct((n_grp, seq_len, _VH, LANES), _F32),
                   jax.ShapeDtypeStruct((n_grp, A_HEAD_DIM, _VH, LANES), _F32)],
        scratch_shapes=[pltpu.VMEM((A_HEAD_DIM, _VH, LANES), _F32)],
        compiler_params=_cparams("parallel", "arbitrary"), name="rwkv_scan",
    )(*args)


def _post_kernel(o_ref, g_ref, bonus_ref, lng_ref, lnb_ref, out_ref, *, tb):
    o3 = o_ref[...].reshape(A_HEAD_DIM, A_HEADS, tb)
    m = jnp.mean(o3, axis=0, keepdims=True)
    oc = o3 - m
    var = jnp.mean(oc * oc, axis=0, keepdims=True)
    on = (oc * lax.rsqrt(var + GN_EPS_A)).reshape(A_WIDTH, tb)
    y = (on * lng_ref[...] + lnb_ref[...] + bonus_ref[...]) * g_ref[...]
    out_ref[...] = y.T


def _post(ot, g, bonus, lng, lnb, tb):
    n = g.shape[1]
    cm = pl.BlockSpec((A_WIDTH, tb), lambda i: (0, i))
    const = pl.BlockSpec((A_WIDTH, tb), lambda i: (0, 0))
    if ot.ndim == 3:
        nt = ot.shape[2] // tb
        o_spec = pl.BlockSpec((1, A_WIDTH, tb), lambda i: (i // nt, 0, i % nt))
    else:
        o_spec = cm
    return pl.pallas_call(
        functools.partial(_post_kernel, tb=tb),
        grid=(n // tb,),
        in_specs=[o_spec, cm, cm, const, const],
        out_specs=pl.BlockSpec((tb, A_WIDTH), lambda i: (i, 0)),
        out_shape=jax.ShapeDtypeStruct((n, A_WIDTH), _F32),
        compiler_params=_cparams("parallel"), name="rwkv_post",
    )(ot, g, bonus, lng, lnb)


def _gla_kernel(*refs, nb, chunk, valid, has_s0):
    it = iter(refs)
    p_ref = next(it)
    s0_ref = next(it) if has_s0 else None
    fup_ref = next(it)
    fb_ref = next(it)
    ng_ref = next(it)
    o_ref = next(it)
    sout_ref = next(it)
    s_scr = next(it)

    c_idx = pl.program_id(1)

    @pl.when(c_idx == 0)
    def _():
        if has_s0:
            s_scr[...] = s0_ref[...]
        else:
            s_scr[...] = jnp.zeros_like(s_scr)

    row = lax.broadcasted_iota(jnp.int32, (chunk, chunk), 0)
    col = lax.broadcasted_iota(jnp.int32, (chunk, chunk), 1)
    causal = row >= col
    eye_k = (lax.broadcasted_iota(jnp.int32, (B_DK, B_DK), 0)
             == lax.broadcasted_iota(jnp.int32, (B_DK, B_DK), 1))
    trow = lax.broadcasted_iota(jnp.int32, (chunk, B_KW), 0)
    ng = ng_ref[...]

    for b in range(nb):
        q_all = p_ref[b, :, 0:B_KW]
        k_all = p_ref[b, :, B_KW:2 * B_KW]
        f_lo = p_ref[b, :, 2 * B_KW + 2 * B_WIDTH:GLA_IN_W]
        gk = jax.nn.log_sigmoid(_dot(f_lo, fup_ref[...]) + fb_ref[...]) / GLA_LOGIT_NORM
        if valid < chunk:
            gk = jnp.where(trow < valid, gk, 0.0)
            k_all = jnp.where(trow < valid, k_all, 0.0)
        bc = gk
        s = 1
        while s < chunk:
            bc = bc + jnp.where(trow >= s, pltpu.roll(bc, s, 0), 0.0)
            s *= 2
        e_pos = jnp.exp(bc)
        e_neg = jnp.exp(-bc)
        b_last = bc[chunk - 1:chunk, :]
        e_rem = jnp.exp(b_last - bc)
        e_last = jnp.exp(b_last)
        qt_all = q_all * (B_DK ** -0.5) * e_pos
        kt_all = k_all * e_neg
        kd_all = k_all * e_rem
        for h in range(B_HEADS):
            ks = slice(h * B_DK, (h + 1) * B_DK)
            vs = slice(2 * B_KW + h * B_DV, 2 * B_KW + (h + 1) * B_DV)
            gs = slice(2 * B_KW + B_WIDTH + h * B_DV, 2 * B_KW + B_WIDTH + (h + 1) * B_DV)
            qt = qt_all[:, ks].astype(_BF16)
            kt = kt_all[:, ks].astype(_BF16)
            kd = kd_all[:, ks].astype(_BF16)
            vh = p_ref[b, :, vs]
            vb = vh.astype(_BF16)
            gh = p_ref[b, :, gs]
            st = s_scr[b, h]
            att = lax.dot_general(qt, kt, (((1,), (1,)), ((), ())), preferred_element_type=_F32)
            att = jnp.where(causal, att, 0.0)
            o = (jnp.dot(att.astype(_BF16), vb, preferred_element_type=_F32)
                 + jnp.dot(qt, st.astype(_BF16), preferred_element_type=_F32))
            e_col = jnp.sum(jnp.where(eye_k, jnp.broadcast_to(e_last[:, ks], (B_DK, B_DK)), 0.0),
                            axis=1, keepdims=True)
            s_scr[b, h] = st * e_col + lax.dot_general(
                kd, vb, (((0,), (0,)), ((), ())), preferred_element_type=_F32)
            on = o * lax.rsqrt(jnp.mean(o * o, -1, keepdims=True) + RMS_EPS) * ng
            o_ref[b, :, h * B_DV:(h + 1) * B_DV] = on * (gh * jax.nn.sigmoid(gh))

    @pl.when(c_idx == pl.num_programs(1) - 1)
    def _():
        sout_ref[...] = s_scr[...]


def _gla(pb3, s0, fup, fb, ng, *, nb, chunk, valid):
    n_seq, seq_len = pb3.shape[0], pb3.shape[1]
    has_s0 = s0 is not None
    in_specs = [pl.BlockSpec((nb, chunk, GLA_IN_PAD), lambda i, c: (i, c, 0))]
    args = [pb3]
    if has_s0:
        in_specs.append(pl.BlockSpec((nb, B_HEADS, B_DK, B_DV), lambda i, c: (i, 0, 0, 0)))
        args.append(s0)
    in_specs += [pl.BlockSpec((B_GATE_LORA, B_KW), lambda i, c: (0, 0)),
                 pl.BlockSpec((1, B_KW), lambda i, c: (0, 0)),
                 pl.BlockSpec((1, B_DV), lambda i, c: (0, 0))]
    args += [fup, fb, ng]
    return pl.pallas_call(
        functools.partial(_gla_kernel, nb=nb, chunk=chunk, valid=valid, has_s0=has_s0),
        grid=(n_seq // nb, seq_len // chunk),
        in_specs=in_specs,
        out_specs=[pl.BlockSpec((nb, chunk, B_WIDTH), lambda i, c: (i, c, 0)),
                   pl.BlockSpec((nb, B_HEADS, B_DK, B_DV), lambda i, c: (i, 0, 0, 0))],
        out_shape=[jax.ShapeDtypeStruct((n_seq, seq_len, B_WIDTH), _F32),
                   jax.ShapeDtypeStruct((n_seq, B_HEADS, B_DK, B_DV), _F32)],
        scratch_shapes=[pltpu.VMEM((nb, B_HEADS, B_DK, B_DV), _F32)],
        compiler_params=_cparams("parallel", "arbitrary"), name="gla",
    )(*args)


def _outproj_kernel(x_ref, oa_ref, ob_ref, wa_ref, wb_ref, g_ref, b_ref, y_ref):
    mix = _dot(oa_ref[...], wa_ref[...]) + _dot(ob_ref[...], wb_ref[...])
    y_ref[...] = _layer_norm(ALPHA * x_ref[...] + mix, g_ref[...], b_ref[...])


def _outproj(x, oa, ob, wa, wb, g, b, tm):
    n = x.shape[0]
    rowb = lambda w: pl.BlockSpec((tm, w), lambda i: (i, 0))
    const = lambda r, c: pl.BlockSpec((r, c), lambda i: (0, 0))
    return pl.pallas_call(
        _outproj_kernel,
        grid=(n // tm,),
        in_specs=[rowb(D_MODEL), rowb(A_WIDTH), rowb(B_WIDTH), const(A_WIDTH, D_MODEL),
                  const(B_WIDTH, D_MODEL), const(1, D_MODEL), const(1, D_MODEL)],
        out_specs=rowb(D_MODEL),
        out_shape=jax.ShapeDtypeStruct((n, D_MODEL), _F32),
        compiler_params=_cparams("parallel"), name="outproj_ln",
    )(x, oa, ob, wa, wb, g, b)


def _ffn_kernel(*refs, tm, period):
    it = iter(refs)
    x_ref = next(it)
    p1_ref = next(it) if period is not None else None
    p2_ref = next(it) if period is not None else None
    wup_ref = next(it)
    cw_ref = next(it)
    cb_ref = next(it)
    wdn_ref = next(it)
    g_ref = next(it)
    b_ref = next(it)
    y_ref = next(it)
    gate_ref = next(it)
    carry_ref = next(it) if period is None else None

    x = x_ref[...]
    u = jnp.dot(x.astype(_BF16), wup_ref[...], preferred_element_type=_F32)
    gate = u[:, :D_FF]
    val = u[:, D_FF:]
    row = lax.broadcasted_iota(jnp.int32, (tm, D_FF), 0)
    r1 = pltpu.roll(gate, 1, 0)
    r2 = pltpu.roll(gate, 2, 0)
    if period is None:
        @pl.when(pl.program_id(1) == 0)
        def _():
            carry_ref[...] = jnp.zeros_like(carry_ref)
        c = carry_ref[...]
        row8 = lax.broadcasted_iota(jnp.int32, (SUBLANES, D_FF), 0)
        top1 = jnp.where(row8 < 1, pltpu.roll(c, 1, 0), r1[:SUBLANES])
        top2 = jnp.where(row8 < 2, pltpu.roll(c, 2, 0), r2[:SUBLANES])
        g1 = jnp.concatenate([top1, r1[SUBLANES:]], axis=0)
        g2 = jnp.concatenate([top2, r2[SUBLANES:]], axis=0)
        carry_ref[...] = gate[tm - SUBLANES:]
        gate_ref[0] = gate[tm - SUBLANES:]
    else:
        g1 = jnp.where(row % period >= 1, r1, p1_ref[...])
        g2 = jnp.where(row % period >= 2, r2, p2_ref[...])
        gate_ref[...] = gate
    acc = cb_ref[...] + g2 * cw_ref[0:1, :] + g1 * cw_ref[1:2, :] + gate * cw_ref[2:3, :]
    h = 0.5 * acc * (1.0 + lax.erf(acc * (2.0 ** -0.5))) * val
    f = jnp.dot(h.astype(_BF16), wdn_ref[...], preferred_element_type=_F32)
    y_ref[...] = _layer_norm(ALPHA * x + f, g_ref[...], b_ref[...])


def _ffn(x, p1, p2, wup, cw, cb, wdn, g, b, *, n_seq, seq_len, tm):
    n = x.shape[0]
    prompt = p1 is None
    if prompt:
        nt = seq_len // tm
        grid = (n_seq, nt)
        rmap = lambda bb, t: (bb * nt + t, 0)
        sem = ("parallel", "arbitrary")
    else:
        grid = (n // tm,)
        rmap = lambda i: (i, 0)
        sem = ("parallel",)
    const = lambda r, c: pl.BlockSpec((r, c), lambda *gidx: (0, 0), pipeline_mode=pl.Buffered(1))
    in_specs = [pl.BlockSpec((tm, D_MODEL), rmap)]
    args = [x]
    if not prompt:
        in_specs += [pl.BlockSpec((tm, D_FF), rmap), pl.BlockSpec((tm, D_FF), rmap)]
        args += [p1, p2]
    in_specs += [const(D_MODEL, 2 * D_FF), const(CONV_W, D_FF), const(1, D_FF),
                 const(D_FF, D_MODEL), const(1, D_MODEL), const(1, D_MODEL)]
    args += [wup, cw, cb, wdn, g, b]
    if prompt:
        gate_spec = pl.BlockSpec((1, SUBLANES, D_FF), lambda bb, t: (bb, 0, 0))
        gate_shape = jax.ShapeDtypeStruct((n_seq, SUBLANES, D_FF), _F32)
        scratch = [pltpu.VMEM((SUBLANES, D_FF), _F32)]
    else:
        gate_spec = pl.BlockSpec((tm, D_FF), rmap)
        gate_shape = jax.ShapeDtypeStruct((n, D_FF), _F32)
        scratch = []
    return pl.pallas_call(
        functools.partial(_ffn_kernel, tm=tm, period=None if prompt else seq_len),
        grid=grid, in_specs=in_specs,
        out_specs=[pl.BlockSpec((tm, D_MODEL), rmap), gate_spec],
        out_shape=[jax.ShapeDtypeStruct((n, D_MODEL), _F32), gate_shape],
        scratch_shapes=scratch, compiler_params=_cparams(*sem), name="ffn_ln",
    )(*args)


def _ops_to_scan(ops, n_grp, seq_len):
    x = ops.reshape(6, A_HEAD_DIM, A_HEADS, n_grp, SCAN_GROUP, seq_len)
    zb = jnp.transpose(x[:5], (3, 5, 0, 1, 4, 2)).reshape(n_grp, seq_len, 5 * A_HEAD_DIM, 64)
    zb = jnp.concatenate([zb, zb], axis=-1)
    xv = x[5].reshape(2, _VH, A_HEADS, n_grp, SCAN_GROUP, seq_len)
    zv = jnp.transpose(xv, (3, 5, 1, 0, 4, 2)).reshape(n_grp, seq_len, _VH, LANES)
    return zb, zv


def _scan_to_cm(o, n_grp, seq_len):
    x = o.reshape(n_grp, seq_len, _VH, 2, SCAN_GROUP, A_HEADS)
    return jnp.transpose(x, (3, 2, 5, 0, 4, 1)).reshape(A_WIDTH, n_grp * SCAN_GROUP * seq_len)


def _state_to_scan(s, n_grp):
    x = s.reshape(n_grp, SCAN_GROUP, A_HEADS, 2, _VH, A_HEAD_DIM)
    return jnp.transpose(x, (0, 5, 4, 3, 1, 2)).reshape(n_grp, A_HEAD_DIM, _VH, LANES)


def _scan_to_state(s, n_grp):
    x = s.reshape(n_grp, A_HEAD_DIM, _VH, 2, SCAN_GROUP, A_HEADS)
    return jnp.transpose(x, (0, 4, 5, 3, 2, 1)).reshape(
        n_grp * SCAN_GROUP, A_HEADS, A_HEAD_DIM, A_HEAD_DIM)


def _bcast_cols(p, tb):
    return jnp.broadcast_to(p[:, None], (p.shape[0], tb))


def _layer_weights(P, l, tb_prep, tb_post):
    w_in = P['w_in'][l]
    wta = w_in[:, _COLS_A].T.astype(_BF16)
    wb = jnp.pad(w_in[:, SHIFT_W:], ((0, 0), (0, GLA_IN_PAD - GLA_IN_W))).astype(_BF16)
    perm = _PERM
    vb = P['vres_bias'][l - 1][perm] if l > 0 else jnp.zeros((A_WIDTH,), _F32)
    prm = jnp.stack([P['w0'][l][perm], P['a0'][l][perm], P['k_k'][l][perm], P['k_a'][l][perm],
                     P['r_k'][l].reshape(A_WIDTH)[perm], vb])
    prm = jnp.broadcast_to(prm[:, :, None], (6, A_WIDTH, tb_prep))
    if l > 0:
        v1 = P['vres_down'][l - 1][perm].T.astype(_BF16)
        v2 = P['vres_up'][l - 1][:, perm].T.astype(_BF16)
    else:
        v1 = jnp.zeros((A_MV_LORA, A_WIDTH), _BF16)
        v2 = jnp.zeros((A_WIDTH, A_MV_LORA), _BF16)
    return dict(
        wta=wta, wb=wb,
        mu=_bcast_cols(P['tok_mu'][l][_COLS_A], tb_prep), prm=prm,
        wup=P['w_lora_up'][l][:, perm].T.astype(_BF16),
        aup=P['a_lora_up'][l][:, perm].T.astype(_BF16),
        gup=P['g_lora_up'][l][:, perm].T.astype(_BF16),
        v1=v1, v2=v2,
        lng=_bcast_cols(P['lnx_g'][l][perm], tb_post), lnb=_bcast_cols(P['lnx_b'][l][perm], tb_post),
        fup=P['gla_f_up'][l].astype(_BF16), fb=P['gla_f_bias'][l][None], ng=P['gla_norm_g'][l][None],
        woa=P['w_out'][l][:A_WIDTH][perm].astype(_BF16), wob=P['w_out'][l][A_WIDTH:].astype(_BF16),
        ln1g=P['ln1_g'][l][None], ln1b=P['ln1_b'][l][None],
        w_up=P['w_up'][l].astype(_BF16), cw=P['conv_w'][l], cb=P['conv_b'][l][None],
        w_down=P['w_down'][l].astype(_BF16), ln2g=P['ln2_g'][l][None], ln2b=P['ln2_b'][l][None])


def _trunk(x3, states, P, prompt):
    n_seq, seq_len, _ = x3.shape
    n = n_seq * seq_len
    n_grp = n_seq // SCAN_GROUP
    x = x3.reshape(n, D_MODEL)
    tb = LANES
    if prompt:
        tt, scan_tb, gla_chunk, gla_valid, gla_t, tm_out, tm_ffn = 512, LANES, GLA_CHUNK, GLA_CHUNK, seq_len, 256, 256
    else:
        st_rwkv, st_shift, st_gla, st_conv = states
        tt, scan_tb, gla_chunk, gla_valid, gla_t, tm_out, tm_ffn = 512, seq_len, SUBLANES, seq_len, SUBLANES, 256, 256
    new_rwkv, new_shift, new_gla, new_conv = [], [], [], []
    ops0 = None
    for l in range(DEPTH):
        W = _layer_weights(P, l, tb, tb)
        pat, pb = _proj(x, W['wta'], W['wb'], tt)
        if prompt:
            bnd = None
        else:
            bnd = jnp.repeat(st_shift[l][:, 0, _COLS_A].T, seq_len, axis=1)
        prep_out = _prep(pat, bnd, ops0, W['mu'], W['prm'], W['wup'], W['aup'], W['gup'],
                         W['v1'], W['v2'], n_seq=n_seq, seq_len=seq_len, tb=tb)
        ops, g, bonus = prep_out[:3]
        if l == 0:
            ops0 = ops
        if prompt:
            o_cm, s_fin = _scan_cm(ops, seq_len, scan_tb)
            s_fin = s_fin[None]
            last_cols = prep_out[3][:, :, tb - 1]
        else:
            zb, zv = _ops_to_scan(ops, n_grp, seq_len)
            o, s_fin = _scan(zb, zv, _state_to_scan(st_rwkv[l], n_grp), scan_tb)
            o_cm = _scan_to_cm(o, n_grp, seq_len)
            last_cols = pat[:, seq_len - 1::seq_len].T
        oa = _post(o_cm, g, bonus, W['lng'], W['lnb'], tb)
        pb3 = pb.reshape(n_seq, seq_len, GLA_IN_PAD)
        if gla_t != seq_len:
            pb3 = jnp.pad(pb3, ((0, 0), (0, gla_t - seq_len), (0, 0)))
        ob, s_gla = _gla(pb3, None if prompt else st_gla[l], W['fup'], W['fb'], W['ng'],
                         nb=SCAN_GROUP, chunk=gla_chunk, valid=gla_valid)
        ob = ob[:, :seq_len].reshape(n, B_WIDTH)
        x1 = _outproj(x, oa, ob, W['woa'], W['wob'], W['ln1g'], W['ln1b'], tm_out)
        if prompt:
            p1 = p2 = None
        else:
            past = st_conv[l]
            zero = jnp.zeros((n_seq, 1, D_FF), _F32)
            p1 = jnp.concatenate([past[:, 1:2], zero, zero, zero], axis=1).reshape(n, D_FF)
            p2 = jnp.concatenate([past[:, 0:1], past[:, 1:2], zero, zero], axis=1).reshape(n, D_FF)
        x, gate = _ffn(x1, p1, p2, W['w_up'], W['cw'], W['cb'], W['w_down'], W['ln2g'], W['ln2b'],
                       n_seq=n_seq, seq_len=seq_len, tm=tm_ffn)
        new_rwkv.append(_scan_to_state(s_fin, n_grp))
        new_shift.append(last_cols[:, _INV_COLS_A][:, None, :])
        new_gla.append(s_gla)
        if prompt:
            new_conv.append(gate[:, SUBLANES - (CONV_W - 1):])
        else:
            new_conv.append(gate.reshape(n_seq, seq_len, D_FF)[:, seq_len - (CONV_W - 1):])
    return (x.reshape(n_seq, seq_len, D_MODEL), jnp.stack(new_rwkv), jnp.stack(new_shift),
            jnp.stack(new_gla), jnp.stack(new_conv))


def kernel(x_prompt, x_sample, state_rwkv, state_shift, state_gla, state_conv, w_in, tok_mu, w0, w_lora_up, a0, a_lora_up, g_lora_up, k_k, k_a, r_k, lnx_g, lnx_b, vres_bias, vres_down, vres_up, gla_f_up, gla_f_bias, gla_norm_g, w_out, ln1_g, ln1_b, w_up, conv_w, conv_b, w_down, ln2_g, ln2_b):
    P = dict(w_in=w_in, tok_mu=tok_mu, w0=w0, w_lora_up=w_lora_up, a0=a0, a_lora_up=a_lora_up,
             g_lora_up=g_lora_up, k_k=k_k, k_a=k_a, r_k=r_k, lnx_g=lnx_g, lnx_b=lnx_b,
             vres_bias=vres_bias, vres_down=vres_down, vres_up=vres_up, gla_f_up=gla_f_up,
             gla_f_bias=gla_f_bias, gla_norm_g=gla_norm_g, w_out=w_out, ln1_g=ln1_g, ln1_b=ln1_b,
             w_up=w_up, conv_w=conv_w, conv_b=conv_b, w_down=w_down, ln2_g=ln2_g, ln2_b=ln2_b)
    y_p, rwkv_p, shift_p, gla_p, conv_p = _trunk(x_prompt, None, P, True)
    y_s, rwkv_s, shift_s, gla_s, conv_s = _trunk(
        x_sample, (state_rwkv, state_shift, state_gla, state_conv), P, False)
    return (y_p, y_s, rwkv_p, rwkv_s, shift_p, shift_s, gla_p, gla_s, conv_p, conv_s)
```

```python
import functools

import numpy as np
import jax
import jax.numpy as jnp
from jax import lax
from jax.experimental import pallas as pl
from jax.experimental.pallas import tpu as pltpu

D_MODEL = 1024
DEPTH = 4
A_WIDTH = 512
A_HEAD_DIM = 64
A_HEADS = 8
A_DECAY_LORA = 64
A_AAA_LORA = 64
A_MV_LORA = 32
A_GATE_LORA = 128
GN_EPS_A = 64e-5
B_WIDTH = 512
B_HEADS = 4
B_DV = 128
B_DK = 64
B_KW = 256
B_GATE_LORA = 16
GLA_LOGIT_NORM = 16.0
GLA_CHUNK = 64
RMS_EPS = 1e-5
D_FF = 2816
CONV_W = 3
ALPHA = (2 * DEPTH) ** 0.25
LN_EPS = 1e-5
SHIFT_W = 3 * A_WIDTH + A_DECAY_LORA + A_AAA_LORA + A_GATE_LORA
GLA_IN_W = 2 * B_KW + 2 * B_WIDTH + B_GATE_LORA
GLA_IN_PAD = 1664

LANES = 128
SUBLANES = 8
SCAN_GROUP = 8
VMEM_LIMIT = 56 * 1024 * 1024

_PERM = np.array([(c % A_HEADS) * A_HEAD_DIM + c // A_HEADS for c in range(A_WIDTH)], np.int32)
_COLS_A = np.concatenate([_PERM, A_WIDTH + _PERM, 2 * A_WIDTH + _PERM,
                          np.arange(3 * A_WIDTH, SHIFT_W, dtype=np.int32)])
_INV_COLS_A = np.argsort(_COLS_A).astype(np.int32)

_F32 = jnp.float32
_BF16 = jnp.bfloat16


def _cparams(*sem):
    return pltpu.CompilerParams(dimension_semantics=sem, vmem_limit_bytes=VMEM_LIMIT)


def _dot(a, b):
    return jnp.dot(a.astype(_BF16), b.astype(_BF16), preferred_element_type=_F32)


def _layer_norm(y, g, b):
    mu = jnp.mean(y, -1, keepdims=True)
    yc = y - mu
    var = jnp.mean(yc * yc, -1, keepdims=True)
    return yc * lax.rsqrt(var + LN_EPS) * g + b


def _proj_kernel(x_ref, wta_ref, wb_ref, pat_ref, pb_ref):
    x = x_ref[...].astype(_BF16)
    pat_ref[...] = lax.dot_general(wta_ref[...], x, (((1,), (1,)), ((), ())),
                                   preferred_element_type=_F32)
    pb_ref[...] = jnp.dot(x, wb_ref[...], preferred_element_type=_F32)


def _proj(x, wta, wb, tt):
    n = x.shape[0]
    return pl.pallas_call(
        _proj_kernel,
        grid=(n // tt,),
        in_specs=[pl.BlockSpec((tt, D_MODEL), lambda i: (i, 0)),
                  pl.BlockSpec((SHIFT_W, D_MODEL), lambda i: (0, 0)),
                  pl.BlockSpec((D_MODEL, GLA_IN_PAD), lambda i: (0, 0))],
        out_specs=[pl.BlockSpec((SHIFT_W, tt), lambda i: (0, i)),
                   pl.BlockSpec((tt, GLA_IN_PAD), lambda i: (i, 0))],
        out_shape=[jax.ShapeDtypeStruct((SHIFT_W, n), _F32),
                   jax.ShapeDtypeStruct((n, GLA_IN_PAD), _F32)],
        compiler_params=_cparams("parallel"),
        name="proj",
    )(x, wta, wb)


def _head_sum(x, tb):
    return jnp.sum(x.reshape(A_HEAD_DIM, A_HEADS, tb), axis=0)


def _head_bcast(s, tb):
    return jnp.broadcast_to(s[None], (A_HEAD_DIM, A_HEADS, tb)).reshape(A_WIDTH, tb)


def _prep_kernel(*refs, tb, period, has_vmix):
    it = iter(refs)
    pat_ref = next(it)
    bnd_ref = next(it) if period is not None else None
    vfirst_ref = next(it) if has_vmix else None
    mu_ref = next(it)
    prm_ref = next(it)
    wup_ref = next(it)
    aup_ref = next(it)
    gup_ref = next(it)
    v1_ref = next(it)
    v2_ref = next(it)
    ops_ref = next(it)
    g_ref = next(it)
    bonus_ref = next(it)
    last_ref = next(it) if period is None else None
    carry_ref = next(it) if period is None else None

    pa = pat_ref[...]
    lane = lax.broadcasted_iota(jnp.int32, (SHIFT_W, tb), 1)
    rolled = pltpu.roll(pa, 1, 1)
    if period is None:
        @pl.when(pl.program_id(1) == 0)
        def _():
            carry_ref[...] = jnp.zeros_like(carry_ref)
        prev = jnp.where(lane == 0, pltpu.roll(carry_ref[...], 1, 1), rolled)
        carry_ref[...] = pa
        last_ref[0] = pa
    else:
        prev = jnp.where(lane % period == 0, bnd_ref[...], rolled)

    xs = pa + (prev - pa) * mu_ref[...]
    r = xs[0:A_WIDTH]
    k = xs[A_WIDTH:2 * A_WIDTH]
    v = xs[2 * A_WIDTH:3 * A_WIDTH]
    o0 = 3 * A_WIDTH
    w_lo = xs[o0:o0 + A_DECAY_LORA]
    a_lo = xs[o0 + A_DECAY_LORA:o0 + A_DECAY_LORA + A_AAA_LORA]
    g_lo = xs[o0 + A_DECAY_LORA + A_AAA_LORA:SHIFT_W]

    w0, a0, k_k, k_a, r_k = prm_ref[0], prm_ref[1], prm_ref[2], prm_ref[3], prm_ref[4]
    w = -jax.nn.softplus(-(w0 + _dot(wup_ref[...], jnp.tanh(w_lo)))) - 0.5
    log_decay = -jnp.exp(w)
    decay = log_decay if period is None else jnp.exp(log_decay)
    a = jax.nn.sigmoid(a0 + _dot(aup_ref[...], a_lo))
    g = _dot(gup_ref[...], jax.nn.sigmoid(g_lo))
    if has_vmix:
        mix = jax.nn.sigmoid(prm_ref[5] + _dot(v2_ref[...], _dot(v1_ref[...], v)))
        v = v + (vfirst_ref[0] - v) * mix
    kk = k * k_k
    ss = _head_sum(kk * kk, tb)
    kk = kk * _head_bcast(lax.rsqrt(jnp.maximum(ss, 1e-24)), tb)
    kx = k * (1.0 + (a - 1.0) * k_a)
    ops_ref[0] = r
    ops_ref[1] = kk
    ops_ref[2] = decay
    ops_ref[3] = kk * a
    ops_ref[4] = kx
    ops_ref[5] = v
    g_ref[...] = g
    bonus_ref[...] = _head_bcast(_head_sum(r * kx * r_k, tb), tb) * v


def _prep(pat, bnd, vfirst_ops, mu, prm, wup, aup, gup, v1, v2, *, n_seq, seq_len, tb):
    n = pat.shape[1]
    prompt = bnd is None
    has_vmix = vfirst_ops is not None
    if prompt:
        nt = seq_len // tb
        grid = (n_seq, nt)
        tok = lambda b, t: b * nt + t
        sem = ("parallel", "arbitrary")
    else:
        grid = (n // tb,)
        tok = lambda i: i
        sem = ("parallel",)
    cmap = lambda f: (lambda *g: f(tok(*g)))
    in_specs = [pl.BlockSpec((SHIFT_W, tb), cmap(lambda j: (0, j)))]
    args = [pat]
    if not prompt:
        in_specs.append(pl.BlockSpec((SHIFT_W, tb), cmap(lambda j: (0, j))))
        args.append(bnd)
    if has_vmix:
        in_specs.append(pl.BlockSpec((1, A_WIDTH, tb), cmap(lambda j: (5, 0, j))))
        args.append(vfirst_ops)
    const2 = lambda *g: (0, 0)
    const3 = lambda *g: (0, 0, 0)
    in_specs += [pl.BlockSpec((SHIFT_W, tb), const2),
                 pl.BlockSpec((6, A_WIDTH, tb), const3),
                 pl.BlockSpec((A_WIDTH, A_DECAY_LORA), const2),
                 pl.BlockSpec((A_WIDTH, A_AAA_LORA), const2),
                 pl.BlockSpec((A_WIDTH, A_GATE_LORA), const2),
                 pl.BlockSpec((A_MV_LORA, A_WIDTH), const2),
                 pl.BlockSpec((A_WIDTH, A_MV_LORA), const2)]
    args += [mu, prm, wup, aup, gup, v1, v2]
    out_specs = [pl.BlockSpec((6, A_WIDTH, tb), cmap(lambda j: (0, 0, j))),
                 pl.BlockSpec((A_WIDTH, tb), cmap(lambda j: (0, j))),
                 pl.BlockSpec((A_WIDTH, tb), cmap(lambda j: (0, j)))]
    out_shape = [jax.ShapeDtypeStruct((6, A_WIDTH, n), _F32),
                 jax.ShapeDtypeStruct((A_WIDTH, n), _F32),
                 jax.ShapeDtypeStruct((A_WIDTH, n), _F32)]
    scratch = [pltpu.VMEM((SHIFT_W, tb), _F32)] if prompt else []
    if prompt:
        out_specs.append(pl.BlockSpec((1, SHIFT_W, tb), lambda b, t: (b, 0, 0)))
        out_shape.append(jax.ShapeDtypeStruct((n_seq, SHIFT_W, tb), _F32))
    return pl.pallas_call(
        functools.partial(_prep_kernel, tb=tb, period=None if prompt else seq_len,
                          has_vmix=has_vmix),
        grid=grid, in_specs=in_specs, out_specs=out_specs, out_shape=out_shape,
        scratch_shapes=scratch, compiler_params=_cparams(*sem), name="rwkv_prep",
    )(*args)


_VH = A_HEAD_DIM // 2


_OP_R, _OP_KK, _OP_DECAY, _OP_B, _OP_K, _OP_V = range(6)


def _scan_steps(s_scr, row, load_v, store_o, n_steps):
    def step(t, carry):
        acc = [jnp.zeros((_VH, LANES), _F32), jnp.zeros((_VH, LANES), _F32)]
        for k in range(A_HEAD_DIM):
            acc[k % 2] = acc[k % 2] + s_scr[k] * row(t, _OP_KK, k)
        sa = -(acc[0] + acc[1])
        vv = load_v(t)
        out = [jnp.zeros((_VH, LANES), _F32), jnp.zeros((_VH, LANES), _F32)]
        for k in range(A_HEAD_DIM):
            s_new = (s_scr[k] * row(t, _OP_DECAY, k) + sa * row(t, _OP_B, k)
                     + vv * row(t, _OP_K, k))
            s_scr[k] = s_new
            out[k % 2] = out[k % 2] + s_new * row(t, _OP_R, k)
        store_o(t, out[0] + out[1])
        return carry

    lax.fori_loop(0, n_steps, step, 0)


def _scan_kernel(*refs, tb, has_s0):
    it = iter(refs)
    zb_ref = next(it)
    zv_ref = next(it)
    s0_ref = next(it) if has_s0 else None
    o_ref = next(it)
    sout_ref = next(it)
    s_scr = next(it)

    t_blk = pl.program_id(1)

    @pl.when(t_blk == 0)
    def _():
        if has_s0:
            s_scr[...] = s0_ref[0]
        else:
            s_scr[...] = jnp.zeros_like(s_scr)

    def row(t, op, k):
        return zb_ref[0, t, pl.ds(op * A_HEAD_DIM + k, 1), :]

    def store_o(t, val):
        o_ref[0, t] = val

    _scan_steps(s_scr, row, lambda t: zv_ref[0, t], store_o, tb)

    @pl.when(t_blk == pl.num_programs(1) - 1)
    def _():
        sout_ref[0] = s_scr[...]


_SCAN_WIN = 32
_PHASE_OPS = (_OP_DECAY, _OP_R, _OP_KK, _OP_B, _OP_K, _OP_V)


def _scan_cm_kernel(*refs, tb):
    in_refs = refs[:SCAN_GROUP]
    o_ref, sout_ref, z_scr, zv_scr, o_scr, s_scr, d_scr = refs[SCAN_GROUP:]
    t_blk = pl.program_id(0)
    phase = pl.program_id(1)
    n_win = tb // _SCAN_WIN

    @pl.when(jnp.logical_and(t_blk == 0, phase == 0))
    def _():
        s_scr[...] = jnp.zeros_like(s_scr)

    def gather(k):
        rows = slice(k * A_HEADS, (k + 1) * A_HEADS)
        return jnp.concatenate([r[0, rows, :] for r in in_refs], axis=0)

    @pl.when(phase == 0)
    def _():
        kg = 8
        nrow = SCAN_GROUP * A_HEADS
        lane = lax.broadcasted_iota(jnp.int32, (kg * nrow, tb), 1) % _SCAN_WIN
        for k0 in range(0, A_HEAD_DIM, kg):
            logd = jnp.concatenate([gather(k0 + j) for j in range(kg)], axis=0)
            c = logd
            s = 1
            while s < _SCAN_WIN:
                c = c + jnp.where(lane >= s, pltpu.roll(c, s, 1), 0.0)
                s *= 2
            dm = jnp.exp(c)
            di = jnp.exp(-c)
            dp = jnp.exp(c - logd)
            for j in range(kg):
                rows = slice(j * nrow, (j + 1) * nrow)
                d_scr[k0 + j] = dm[rows]
                d_scr[A_HEAD_DIM + k0 + j] = di[rows]
                d_scr[2 * A_HEAD_DIM + k0 + j] = dp[rows]
                z_scr[k0 + j] = jnp.concatenate([dm[rows], dm[rows]], axis=0).T

    @pl.when(jnp.logical_and(phase >= 1, phase <= 4))
    def _():
        fac = jnp.where(phase == 1, 0, jnp.where(phase == 2, 2, 1)) * A_HEAD_DIM
        for k in range(A_HEAD_DIM):
            m = gather(k) * d_scr[fac + k]
            z_scr[phase * A_HEAD_DIM + k] = jnp.concatenate([m, m], axis=0).T

    @pl.when(phase == 5)
    def _():
        for v in range(_VH):
            lo = slice(v * A_HEADS, (v + 1) * A_HEADS)
            hi = slice((_VH + v) * A_HEADS, (_VH + v + 1) * A_HEADS)
            m = jnp.concatenate([r[0, lo, :] for r in in_refs]
                                + [r[0, hi, :] for r in in_refs], axis=0)
            zv_scr[pl.ds(v * tb, tb), :] = m.T

    @pl.when(phase == 6)
    def _():
        def row(t, slot, k):
            return z_scr[slot * A_HEAD_DIM + k, pl.ds(t, 1), :]

        zero = jnp.zeros((_VH, LANES), _F32)

        def window(w, carry):
            t0 = w * _SCAN_WIN
            acc = [zero, zero]
            for k in range(A_HEAD_DIM):
                acc[k % 2] = acc[k % 2] + s_scr[k] * row(t0, 2, k)

            def step(i, nacc):
                t = t0 + i
                tn = jnp.minimum(t + 1, tb - 1)
                sa = -nacc
                vv = zv_scr[pl.ds(t, _VH, stride=tb), :]
                out = [zero, zero]
                nxt = [zero, zero]
                for k in range(A_HEAD_DIM):
                    s_new = s_scr[k] + sa * row(t, 3, k) + vv * row(t, 4, k)
                    s_scr[k] = s_new
                    out[k % 2] = out[k % 2] + s_new * row(t, 1, k)
                    nxt[k % 2] = nxt[k % 2] + s_new * row(tn, 2, k)
                o_scr[pl.ds(pl.multiple_of(t * _VH, _VH), _VH), :] = out[0] + out[1]
                return nxt[0] + nxt[1]

            lax.fori_loop(0, _SCAN_WIN, step, acc[0] + acc[1])
            t_end = t0 + _SCAN_WIN - 1
            for k in range(A_HEAD_DIM):
                s_scr[k] = s_scr[k] * row(t_end, 0, k)
            return carry

        lax.fori_loop(0, n_win, window, 0)
        for v in range(_VH):
            mt = o_scr[pl.ds(v, tb, stride=_VH), :].T
            for half in range(2):
                for b in range(SCAN_GROUP):
                    src = (half * SCAN_GROUP + b) * A_HEADS
                    dst = (half * _VH + v) * A_HEADS
                    o_ref[b, dst:dst + A_HEADS, :] = mt[src:src + A_HEADS, :]
        sout_ref[...] = s_scr[...]


def _scan_cm(ops, seq_len, tb):
    nt = seq_len // tb
    n_phase = len(_PHASE_OPS) + 1

    def op_of_phase(p):
        op = _PHASE_OPS[-1]
        for i in range(len(_PHASE_OPS) - 2, -1, -1):
            op = jnp.where(p == i, _PHASE_OPS[i], op)
        return op

    def in_spec(b):
        return pl.BlockSpec((1, A_WIDTH, tb), lambda t, p: (op_of_phase(p), 0, b * nt + t))

    return pl.pallas_call(
        functools.partial(_scan_cm_kernel, tb=tb),
        grid=(nt, n_phase),
        in_specs=[in_spec(b) for b in range(SCAN_GROUP)],
        out_specs=[pl.BlockSpec((SCAN_GROUP, A_WIDTH, tb), lambda t, p: (0, 0, t)),
                   pl.BlockSpec((A_HEAD_DIM, _VH, LANES), lambda t, p: (0, 0, 0))],
        out_shape=[jax.ShapeDtypeStruct((SCAN_GROUP, A_WIDTH, seq_len), _F32),
                   jax.ShapeDtypeStruct((A_HEAD_DIM, _VH, LANES), _F32)],
        scratch_shapes=[pltpu.VMEM((5 * A_HEAD_DIM, tb, LANES), _F32),
                        pltpu.VMEM((_VH * tb, LANES), _F32),
                        pltpu.VMEM((tb * _VH, LANES), _F32),
                        pltpu.VMEM((A_HEAD_DIM, _VH, LANES), _F32),
                        pltpu.VMEM((3 * A_HEAD_DIM, SCAN_GROUP * A_HEADS, tb), _F32)],
        compiler_params=_cparams("arbitrary", "arbitrary"), name="rwkv_scan_cm",
    )(*([ops] * SCAN_GROUP))


def _scan(zb, zv, s0, tb):
    n_grp, seq_len = zb.shape[0], zb.shape[1]
    has_s0 = s0 is not None
    in_specs = [pl.BlockSpec((1, tb, 5 * A_HEAD_DIM, LANES), lambda g, t: (g, t, 0, 0)),
                pl.BlockSpec((1, tb, _VH, LANES), lambda g, t: (g, t, 0, 0))]
    args = [zb, zv]
    if has_s0:
        in_specs.append(pl.BlockSpec((1, A_HEAD_DIM, _VH, LANES), lambda g, t: (g, 0, 0, 0)))
        args.append(s0)
    return pl.pallas_call(
        functools.partial(_scan_kernel, tb=tb, has_s0=has_s0),
        grid=(n_grp, seq_len // tb),
        in_specs=in_specs,
        out_specs=[pl.BlockSpec((1, tb, _VH, LANES), lambda g, t: (g, t, 0, 0)),
                   pl.BlockSpec((1, A_HEAD_DIM, _VH, LANES), lambda g, t: (g, 0, 0, 0))],
        out_shape=[jax.ShapeDtypeStruct((n_grp, seq_len, _VH, LANES), _F32),
                   jax.ShapeDtypeStruct((n_grp, A_HEAD_DIM, _VH, LANES), _F32)],
        scratch_shapes=[pltpu.VMEM((A_HEAD_DIM, _VH, LANES), _F32)],
        compiler_params=_cparams("parallel", "arbitrary"), name="rwkv_scan",
    )(*args)


def _gla_kernel(*refs, nb, chunk, valid, has_s0):
    it = iter(refs)
    p_ref = next(it)
    s0_ref = next(it) if has_s0 else None
    fup_ref = next(it)
    fb_ref = next(it)
    ng_ref = next(it)
    o_ref = next(it)
    sout_ref = next(it)
    s_scr = next(it)

    c_idx = pl.program_id(1)

    @pl.when(c_idx == 0)
    def _():
        if has_s0:
            s_scr[...] = s0_ref[...]
        else:
            s_scr[...] = jnp.zeros_like(s_scr)

    row = lax.broadcasted_iota(jnp.int32, (chunk, chunk), 0)
    col = lax.broadcasted_iota(jnp.int32, (chunk, chunk), 1)
    causal = row >= col
    eye_k = (lax.broadcasted_iota(jnp.int32, (B_DK, B_DK), 0)
             == lax.broadcasted_iota(jnp.int32, (B_DK, B_DK), 1))
    rows = nb * chunk
    trow = lax.broadcasted_iota(jnp.int32, (rows, B_KW), 0) % chunk
    ng = ng_ref[...]

    q_all = p_ref[:, :, 0:B_KW].reshape(rows, B_KW)
    k_all = p_ref[:, :, B_KW:2 * B_KW].reshape(rows, B_KW)
    f_lo = p_ref[:, :, 2 * B_KW + 2 * B_WIDTH:GLA_IN_W].reshape(rows, B_GATE_LORA)
    gk = jax.nn.log_sigmoid(_dot(f_lo, fup_ref[...]) + fb_ref[...]) / GLA_LOGIT_NORM
    if valid < chunk:
        gk = jnp.where(trow < valid, gk, 0.0)
        k_all = jnp.where(trow < valid, k_all, 0.0)
    bc = gk
    s = 1
    while s < chunk:
        bc = bc + jnp.where(trow >= s, pltpu.roll(bc, s, 0), 0.0)
        s *= 2
    b_last3 = bc.reshape(nb, chunk, B_KW)[:, chunk - 1:chunk, :]
    b_last = jnp.broadcast_to(b_last3, (nb, chunk, B_KW)).reshape(rows, B_KW)
    e_last3 = jnp.exp(b_last3)
    qt_all = q_all * (B_DK ** -0.5) * jnp.exp(bc)
    kt_all = k_all * jnp.exp(-bc)
    kd_all = k_all * jnp.exp(b_last - bc)

    units = [(b, h) for b in range(nb) for h in range(B_HEADS)]
    group = 16
    for u0 in range(0, len(units), group):
        grp = units[u0:u0 + group]
        att, qs, kv, vbs, sts = [], [], [], [], []
        for b, h in grp:
            rs = slice(b * chunk, (b + 1) * chunk)
            ks = slice(h * B_DK, (h + 1) * B_DK)
            vs = slice(2 * B_KW + h * B_DV, 2 * B_KW + (h + 1) * B_DV)
            qt = qt_all[rs, ks].astype(_BF16)
            kt = kt_all[rs, ks].astype(_BF16)
            kd = kd_all[rs, ks].astype(_BF16)
            vb = p_ref[b, :, vs].astype(_BF16)
            st = s_scr[b, h]
            att.append(lax.dot_general(qt, kt, (((1,), (1,)), ((), ())),
                                       preferred_element_type=_F32))
            qs.append(jnp.dot(qt, st.astype(_BF16), preferred_element_type=_F32))
            kv.append(lax.dot_general(kd, vb, (((0,), (0,)), ((), ())),
                                      preferred_element_type=_F32))
            vbs.append(vb)
            sts.append(st)
        outs = []
        for i in range(len(grp)):
            a = jnp.where(causal, att[i], 0.0).astype(_BF16)
            outs.append(jnp.dot(a, vbs[i], preferred_element_type=_F32) + qs[i])
        for i, (b, h) in enumerate(grp):
            ks = slice(h * B_DK, (h + 1) * B_DK)
            gs = slice(2 * B_KW + B_WIDTH + h * B_DV, 2 * B_KW + B_WIDTH + (h + 1) * B_DV)
            e_col = jnp.sum(jnp.where(eye_k, jnp.broadcast_to(e_last3[b, :, ks], (B_DK, B_DK)), 0.0),
                            axis=1, keepdims=True)
            s_scr[b, h] = sts[i] * e_col + kv[i]
            o = outs[i]
            gh = p_ref[b, :, gs]
            on = o * lax.rsqrt(jnp.mean(o * o, -1, keepdims=True) + RMS_EPS) * ng
            o_ref[b, :, h * B_DV:(h + 1) * B_DV] = on * (gh * jax.nn.sigmoid(gh))

    @pl.when(c_idx == pl.num_programs(1) - 1)
    def _():
        sout_ref[...] = s_scr[...]


def _gla(pb3, s0, fup, fb, ng, *, nb, chunk, valid):
    n_seq, seq_len = pb3.shape[0], pb3.shape[1]
    has_s0 = s0 is not None
    in_specs = [pl.BlockSpec((nb, chunk, GLA_IN_PAD), lambda i, c: (i, c, 0))]
    args = [pb3]
    if has_s0:
        in_specs.append(pl.BlockSpec((nb, B_HEADS, B_DK, B_DV), lambda i, c: (i, 0, 0, 0)))
        args.append(s0)
    in_specs += [pl.BlockSpec((B_GATE_LORA, B_KW), lambda i, c: (0, 0)),
                 pl.BlockSpec((1, B_KW), lambda i, c: (0, 0)),
                 pl.BlockSpec((1, B_DV), lambda i, c: (0, 0))]
    args += [fup, fb, ng]
    return pl.pallas_call(
        functools.partial(_gla_kernel, nb=nb, chunk=chunk, valid=valid, has_s0=has_s0),
        grid=(n_seq // nb, seq_len // chunk),
        in_specs=in_specs,
        out_specs=[pl.BlockSpec((nb, chunk, B_WIDTH), lambda i, c: (i, c, 0)),
                   pl.BlockSpec((nb, B_HEADS, B_DK, B_DV), lambda i, c: (i, 0, 0, 0))],
        out_shape=[jax.ShapeDtypeStruct((n_seq, seq_len, B_WIDTH), _F32),
                   jax.ShapeDtypeStruct((n_seq, B_HEADS, B_DK, B_DV), _F32)],
        scratch_shapes=[pltpu.VMEM((nb, B_HEADS, B_DK, B_DV), _F32)],
        compiler_params=_cparams("parallel", "arbitrary"), name="gla",
    )(*args)


def _outproj_kernel(x_ref, o_ref, gate_ref, bonus_ref, lng_ref, lnb_ref, ob_ref, wa_ref, wb_ref,
                    g_ref, b_ref, y_ref, *, tm):
    o3 = o_ref[...].reshape(A_HEAD_DIM, A_HEADS, tm)
    m = jnp.mean(o3, axis=0, keepdims=True)
    oc = o3 - m
    var = jnp.mean(oc * oc, axis=0, keepdims=True)
    on = (oc * lax.rsqrt(var + GN_EPS_A)).reshape(A_WIDTH, tm)
    oa = ((on * lng_ref[...] + lnb_ref[...] + bonus_ref[...]) * gate_ref[...]).T
    mix = _dot(oa, wa_ref[...]) + _dot(ob_ref[...], wb_ref[...])
    y_ref[...] = _layer_norm(ALPHA * x_ref[...] + mix, g_ref[...], b_ref[...])


def _outproj(x, o_cm, gate, bonus, lng, lnb, ob, wa, wb, g, b, tm):
    n = x.shape[0]
    rowb = lambda w: pl.BlockSpec((tm, w), lambda i: (i, 0))
    const = lambda r, c: pl.BlockSpec((r, c), lambda i: (0, 0))
    cm = pl.BlockSpec((A_WIDTH, tm), lambda i: (0, i))
    if o_cm.ndim == 3:
        nt = o_cm.shape[2] // tm
        o_spec = pl.BlockSpec((1, A_WIDTH, tm), lambda i: (i // nt, 0, i % nt))
    else:
        o_spec = cm
    return pl.pallas_call(
        functools.partial(_outproj_kernel, tm=tm),
        grid=(n // tm,),
        in_specs=[rowb(D_MODEL), o_spec, cm, cm, const(A_WIDTH, tm), const(A_WIDTH, tm),
                  rowb(B_WIDTH), const(A_WIDTH, D_MODEL), const(B_WIDTH, D_MODEL),
                  const(1, D_MODEL), const(1, D_MODEL)],
        out_specs=rowb(D_MODEL),
        out_shape=jax.ShapeDtypeStruct((n, D_MODEL), _F32),
        compiler_params=_cparams("parallel"), name="outproj_ln",
    )(x, o_cm, gate, bonus, lng, lnb, ob, wa, wb, g, b)


def _ffn_kernel(*refs, tm, period):
    it = iter(refs)
    x_ref = next(it)
    p1_ref = next(it) if period is not None else None
    p2_ref = next(it) if period is not None else None
    wup_ref = next(it)
    cw_ref = next(it)
    cb_ref = next(it)
    wdn_ref = next(it)
    g_ref = next(it)
    b_ref = next(it)
    y_ref = next(it)
    gate_ref = next(it)
    carry_ref = next(it) if period is None else None

    x = x_ref[...]
    u = jnp.dot(x.astype(_BF16), wup_ref[...], preferred_element_type=_F32)
    gate = u[:, :D_FF]
    val = u[:, D_FF:]
    row = lax.broadcasted_iota(jnp.int32, (tm, D_FF), 0)
    r1 = pltpu.roll(gate, 1, 0)
    r2 = pltpu.roll(gate, 2, 0)
    if period is None:
        @pl.when(pl.program_id(1) == 0)
        def _():
            carry_ref[...] = jnp.zeros_like(carry_ref)
        c = carry_ref[...]
        row8 = lax.broadcasted_iota(jnp.int32, (SUBLANES, D_FF), 0)
        top1 = jnp.where(row8 < 1, pltpu.roll(c, 1, 0), r1[:SUBLANES])
        top2 = jnp.where(row8 < 2, pltpu.roll(c, 2, 0), r2[:SUBLANES])
        g1 = jnp.concatenate([top1, r1[SUBLANES:]], axis=0)
        g2 = jnp.concatenate([top2, r2[SUBLANES:]], axis=0)
        carry_ref[...] = gate[tm - SUBLANES:]
        gate_ref[0] = gate[tm - SUBLANES:]
    else:
        g1 = jnp.where(row % period >= 1, r1, p1_ref[...])
        g2 = jnp.where(row % period >= 2, r2, p2_ref[...])
        gate_ref[...] = gate
    acc = cb_ref[...] + g2 * cw_ref[0:1, :] + g1 * cw_ref[1:2, :] + gate * cw_ref[2:3, :]
    h = 0.5 * acc * (1.0 + lax.erf(acc * (2.0 ** -0.5))) * val
    f = jnp.dot(h.astype(_BF16), wdn_ref[...], preferred_element_type=_F32)
    y_ref[...] = _layer_norm(ALPHA * x + f, g_ref[...], b_ref[...])


def _ffn(x, p1, p2, wup, cw, cb, wdn, g, b, *, n_seq, seq_len, tm):
    n = x.shape[0]
    prompt = p1 is None
    if prompt:
        nt = seq_len // tm
        grid = (n_seq, nt)
        rmap = lambda bb, t: (bb * nt + t, 0)
        sem = ("parallel", "arbitrary")
    else:
        grid = (n // tm,)
        rmap = lambda i: (i, 0)
        sem = ("parallel",)
    const = lambda r, c: pl.BlockSpec((r, c), lambda *gidx: (0, 0), pipeline_mode=pl.Buffered(1))
    in_specs = [pl.BlockSpec((tm, D_MODEL), rmap)]
    args = [x]
    if not prompt:
        in_specs += [pl.BlockSpec((tm, D_FF), rmap), pl.BlockSpec((tm, D_FF), rmap)]
        args += [p1, p2]
    in_specs += [const(D_MODEL, 2 * D_FF), const(CONV_W, D_FF), const(1, D_FF),
                 const(D_FF, D_MODEL), const(1, D_MODEL), const(1, D_MODEL)]
    args += [wup, cw, cb, wdn, g, b]
    if prompt:
        gate_spec = pl.BlockSpec((1, SUBLANES, D_FF), lambda bb, t: (bb, 0, 0))
        gate_shape = jax.ShapeDtypeStruct((n_seq, SUBLANES, D_FF), _F32)
        scratch = [pltpu.VMEM((SUBLANES, D_FF), _F32)]
    else:
        gate_spec = pl.BlockSpec((tm, D_FF), rmap)
        gate_shape = jax.ShapeDtypeStruct((n, D_FF), _F32)
        scratch = []
    return pl.pallas_call(
        functools.partial(_ffn_kernel, tm=tm, period=None if prompt else seq_len),
        grid=grid, in_specs=in_specs,
        out_specs=[pl.BlockSpec((tm, D_MODEL), rmap), gate_spec],
        out_shape=[jax.ShapeDtypeStruct((n, D_MODEL), _F32), gate_shape],
        scratch_shapes=scratch, compiler_params=_cparams(*sem), name="ffn_ln",
    )(*args)


def _ops_to_scan(ops, n_grp, seq_len):
    x = ops.reshape(6, A_HEAD_DIM, A_HEADS, n_grp, SCAN_GROUP, seq_len)
    zb = jnp.transpose(x[:5], (3, 5, 0, 1, 4, 2)).reshape(n_grp, seq_len, 5 * A_HEAD_DIM, 64)
    zb = jnp.concatenate([zb, zb], axis=-1)
    xv = x[5].reshape(2, _VH, A_HEADS, n_grp, SCAN_GROUP, seq_len)
    zv = jnp.transpose(xv, (3, 5, 1, 0, 4, 2)).reshape(n_grp, seq_len, _VH, LANES)
    return zb, zv


def _scan_to_cm(o, n_grp, seq_len):
    x = o.reshape(n_grp, seq_len, _VH, 2, SCAN_GROUP, A_HEADS)
    return jnp.transpose(x, (3, 2, 5, 0, 4, 1)).reshape(A_WIDTH, n_grp * SCAN_GROUP * seq_len)


def _state_to_scan(s, n_grp):
    x = s.reshape(n_grp, SCAN_GROUP, A_HEADS, 2, _VH, A_HEAD_DIM)
    return jnp.transpose(x, (0, 5, 4, 3, 1, 2)).reshape(n_grp, A_HEAD_DIM, _VH, LANES)


def _scan_to_state(s, n_grp):
    x = s.reshape(n_grp, A_HEAD_DIM, _VH, 2, SCAN_GROUP, A_HEADS)
    return jnp.transpose(x, (0, 4, 5, 3, 2, 1)).reshape(
        n_grp * SCAN_GROUP, A_HEADS, A_HEAD_DIM, A_HEAD_DIM)


def _bcast_cols(p, tb):
    return jnp.broadcast_to(p[:, None], (p.shape[0], tb))


def _layer_weights(P, l, tb_prep, tb_post):
    w_in = P['w_in'][l]
    wta = w_in[:, _COLS_A].T.astype(_BF16)
    wb = jnp.pad(w_in[:, SHIFT_W:], ((0, 0), (0, GLA_IN_PAD - GLA_IN_W))).astype(_BF16)
    perm = _PERM
    vb = P['vres_bias'][l - 1][perm] if l > 0 else jnp.zeros((A_WIDTH,), _F32)
    prm = jnp.stack([P['w0'][l][perm], P['a0'][l][perm], P['k_k'][l][perm], P['k_a'][l][perm],
                     P['r_k'][l].reshape(A_WIDTH)[perm], vb])
    prm = jnp.broadcast_to(prm[:, :, None], (6, A_WIDTH, tb_prep))
    if l > 0:
        v1 = P['vres_down'][l - 1][perm].T.astype(_BF16)
        v2 = P['vres_up'][l - 1][:, perm].T.astype(_BF16)
    else:
        v1 = jnp.zeros((A_MV_LORA, A_WIDTH), _BF16)
        v2 = jnp.zeros((A_WIDTH, A_MV_LORA), _BF16)
    return dict(
        wta=wta, wb=wb,
        mu=_bcast_cols(P['tok_mu'][l][_COLS_A], tb_prep), prm=prm,
        wup=P['w_lora_up'][l][:, perm].T.astype(_BF16),
        aup=P['a_lora_up'][l][:, perm].T.astype(_BF16),
        gup=P['g_lora_up'][l][:, perm].T.astype(_BF16),
        v1=v1, v2=v2,
        lng=_bcast_cols(P['lnx_g'][l][perm], tb_post), lnb=_bcast_cols(P['lnx_b'][l][perm], tb_post),
        fup=P['gla_f_up'][l].astype(_BF16), fb=P['gla_f_bias'][l][None], ng=P['gla_norm_g'][l][None],
        woa=P['w_out'][l][:A_WIDTH][perm].astype(_BF16), wob=P['w_out'][l][A_WIDTH:].astype(_BF16),
        ln1g=P['ln1_g'][l][None], ln1b=P['ln1_b'][l][None],
        w_up=P['w_up'][l].astype(_BF16), cw=P['conv_w'][l], cb=P['conv_b'][l][None],
        w_down=P['w_down'][l].astype(_BF16), ln2g=P['ln2_g'][l][None], ln2b=P['ln2_b'][l][None])


def _trunk(x3, states, P, prompt):
    n_seq, seq_len, _ = x3.shape
    n = n_seq * seq_len
    n_grp = n_seq // SCAN_GROUP
    x = x3.reshape(n, D_MODEL)
    tb = 2 * LANES
    if prompt:
        tt, scan_tb, gla_chunk, gla_valid, gla_t, tm_out, tm_ffn = 512, LANES, GLA_CHUNK, GLA_CHUNK, seq_len, 256, 256
    else:
        st_rwkv, st_shift, st_gla, st_conv = states
        tt, scan_tb, gla_chunk, gla_valid, gla_t, tm_out, tm_ffn = 512, seq_len, SUBLANES, seq_len, SUBLANES, 256, 256
    new_rwkv, new_shift, new_gla, new_conv = [], [], [], []
    ops0 = None
    for l in range(DEPTH):
        W = _layer_weights(P, l, tb, tm_out)
        pat, pb = _proj(x, W['wta'], W['wb'], tt)
        if prompt:
            bnd = None
        else:
            bnd = jnp.repeat(st_shift[l][:, 0, _COLS_A].T, seq_len, axis=1)
        prep_out = _prep(pat, bnd, ops0, W['mu'], W['prm'], W['wup'], W['aup'], W['gup'],
                         W['v1'], W['v2'], n_seq=n_seq, seq_len=seq_len, tb=tb)
        ops, g, bonus = prep_out[:3]
        if l == 0:
            ops0 = ops
        if prompt:
            o_cm, s_fin = _scan_cm(ops, seq_len, scan_tb)
            s_fin = s_fin[None]
            last_cols = prep_out[3][:, :, tb - 1]
        else:
            zb, zv = _ops_to_scan(ops, n_grp, seq_len)
            o, s_fin = _scan(zb, zv, _state_to_scan(st_rwkv[l], n_grp), scan_tb)
            o_cm = _scan_to_cm(o, n_grp, seq_len)
            last_cols = pat[:, seq_len - 1::seq_len].T
        pb3 = pb.reshape(n_seq, seq_len, GLA_IN_PAD)
        if gla_t != seq_len:
            pb3 = jnp.pad(pb3, ((0, 0), (0, gla_t - seq_len), (0, 0)))
        ob, s_gla = _gla(pb3, None if prompt else st_gla[l], W['fup'], W['fb'], W['ng'],
                         nb=SCAN_GROUP, chunk=gla_chunk, valid=gla_valid)
        ob = ob[:, :seq_len].reshape(n, B_WIDTH)
        x1 = _outproj(x, o_cm, g, bonus, W['lng'], W['lnb'], ob, W['woa'], W['wob'],
                      W['ln1g'], W['ln1b'], tm_out)
        if prompt:
            p1 = p2 = None
        else:
            past = st_conv[l]
            zero = jnp.zeros((n_seq, 1, D_FF), _F32)
            p1 = jnp.concatenate([past[:, 1:2], zero, zero, zero], axis=1).reshape(n, D_FF)
            p2 = jnp.concatenate([past[:, 0:1], past[:, 1:2], zero, zero], axis=1).reshape(n, D_FF)
        x, gate = _ffn(x1, p1, p2, W['w_up'], W['cw'], W['cb'], W['w_down'], W['ln2g'], W['ln2b'],
                       n_seq=n_seq, seq_len=seq_len, tm=tm_ffn)
        new_rwkv.append(_scan_to_state(s_fin, n_grp))
        new_shift.append(last_cols[:, _INV_COLS_A][:, None, :])
        new_gla.append(s_gla)
        if prompt:
            new_conv.append(gate[:, SUBLANES - (CONV_W - 1):])
        else:
            new_conv.append(gate.reshape(n_seq, seq_len, D_FF)[:, seq_len - (CONV_W - 1):])
    return (x.reshape(n_seq, seq_len, D_MODEL), jnp.stack(new_rwkv), jnp.stack(new_shift),
            jnp.stack(new_gla), jnp.stack(new_conv))


def kernel(x_prompt, x_sample, state_rwkv, state_shift, state_gla, state_conv, w_in, tok_mu, w0, w_lora_up, a0, a_lora_up, g_lora_up, k_k, k_a, r_k, lnx_g, lnx_b, vres_bias, vres_down, vres_up, gla_f_up, gla_f_bias, gla_norm_g, w_out, ln1_g, ln1_b, w_up, conv_w, conv_b, w_down, ln2_g, ln2_b):
    P = dict(w_in=w_in, tok_mu=tok_mu, w0=w0, w_lora_up=w_lora_up, a0=a0, a_lora_up=a_lora_up,
             g_lora_up=g_lora_up, k_k=k_k, k_a=k_a, r_k=r_k, lnx_g=lnx_g, lnx_b=lnx_b,
             vres_bias=vres_bias, vres_down=vres_down, vres_up=vres_up, gla_f_up=gla_f_up,
             gla_f_bias=gla_f_bias, gla_norm_g=gla_norm_g, w_out=w_out, ln1_g=ln1_g, ln1_b=ln1_b,
             w_up=w_up, conv_w=conv_w, conv_b=conv_b, w_down=w_down, ln2_g=ln2_g, ln2_b=ln2_b)
    y_p, rwkv_p, shift_p, gla_p, conv_p = _trunk(x_prompt, None, P, True)
    y_s, rwkv_s, shift_s, gla_s, conv_s = _trunk(
        x_sample, (state_rwkv, state_shift, state_gla, state_conv), P, False)
    return (y_p, y_s, rwkv_p, rwkv_s, shift_p, shift_s, gla_p, gla_s, conv_p, conv_s)
```

```python
import functools

import jax
import jax.numpy as jnp
from jax import lax
from jax.experimental import pallas as pl
from jax.experimental.pallas import tpu as pltpu

D_MODEL = 1024
DEPTH = 4
A_WIDTH = 512
A_HEAD_DIM = 64
A_HEADS = 8
A_DECAY_LORA = 64
A_AAA_LORA = 64
A_MV_LORA = 32
A_GATE_LORA = 128
GN_EPS_A = 64e-5
B_WIDTH = 512
B_HEADS = 4
B_DV = 128
B_DK = 64
B_KW = 256
B_GATE_LORA = 16
GLA_LOGIT_NORM = 16.0
GLA_CHUNK = 64
RMS_EPS = 1e-5
D_FF = 2816
CONV_W = 3
ALPHA = (2 * DEPTH) ** 0.25
LN_EPS = 1e-5
SHIFT_W = 3 * A_WIDTH + A_DECAY_LORA + A_AAA_LORA + A_GATE_LORA
GLA_IN_W = 2 * B_KW + 2 * B_WIDTH + B_GATE_LORA
GLA_IN_PAD = 1664

LANES = 128
SUBLANES = 8
SCAN_GROUP = 8
VMEM_LIMIT = 56 * 1024 * 1024


_F32 = jnp.float32
_BF16 = jnp.bfloat16


def _cparams(*sem):
    return pltpu.CompilerParams(dimension_semantics=sem, vmem_limit_bytes=VMEM_LIMIT)


def _dot(a, b):
    return jnp.dot(a.astype(_BF16), b.astype(_BF16), preferred_element_type=_F32)


def _layer_norm(y, g, b):
    mu = jnp.mean(y, -1, keepdims=True)
    yc = y - mu
    var = jnp.mean(yc * yc, -1, keepdims=True)
    return yc * lax.rsqrt(var + LN_EPS) * g + b


def _proj_kernel(x_ref, wta_ref, wb_ref, pat_ref, pb_ref):
    x = x_ref[...].astype(_BF16)
    pat_ref[...] = lax.dot_general(wta_ref[...], x, (((1,), (1,)), ((), ())),
                                   preferred_element_type=_F32)
    pb_ref[...] = jnp.dot(x, wb_ref[...], preferred_element_type=_F32)


def _proj(x, wta, wb, tt):
    n = x.shape[0]
    return pl.pallas_call(
        _proj_kernel,
        grid=(n // tt,),
        in_specs=[pl.BlockSpec((tt, D_MODEL), lambda i: (i, 0)),
                  pl.BlockSpec((SHIFT_W, D_MODEL), lambda i: (0, 0)),
                  pl.BlockSpec((D_MODEL, GLA_IN_PAD), lambda i: (0, 0))],
        out_specs=[pl.BlockSpec((SHIFT_W, tt), lambda i: (0, i)),
                   pl.BlockSpec((tt, GLA_IN_PAD), lambda i: (i, 0))],
        out_shape=[jax.ShapeDtypeStruct((SHIFT_W, n), _F32),
                   jax.ShapeDtypeStruct((n, GLA_IN_PAD), _F32)],
        compiler_params=_cparams("parallel"),
        name="proj",
    )(x, wta, wb)


def _head_sum(x, tb):
    return jnp.sum(x.reshape(A_HEAD_DIM, A_HEADS, tb), axis=0)


def _head_bcast(s, tb):
    return jnp.broadcast_to(s[None], (A_HEAD_DIM, A_HEADS, tb)).reshape(A_WIDTH, tb)


def _prep_kernel(*refs, tb, period, has_vmix):
    it = iter(refs)
    pat_ref = next(it)
    bnd_ref = next(it) if period is not None else None
    vfirst_ref = next(it) if has_vmix else None
    mu_ref = next(it)
    prm_ref = next(it)
    wup_ref = next(it)
    aup_ref = next(it)
    gup_ref = next(it)
    v1_ref = next(it)
    v2_ref = next(it)
    ops_ref = next(it)
    g_ref = next(it)
    bonus_ref = next(it)
    last_ref = next(it) if period is None else None
    carry_ref = next(it) if period is None else None

    pa = pat_ref[...]
    lane = lax.broadcasted_iota(jnp.int32, (SHIFT_W, tb), 1)
    rolled = pltpu.roll(pa, 1, 1)
    if period is None:
        @pl.when(pl.program_id(1) == 0)
        def _():
            carry_ref[...] = jnp.zeros_like(carry_ref)
        prev = jnp.where(lane == 0, pltpu.roll(carry_ref[...], 1, 1), rolled)
        carry_ref[...] = pa
        last_ref[0] = pa
    else:
        prev = jnp.where(lane % period == 0, bnd_ref[...], rolled)

    xs = pa + (prev - pa) * mu_ref[...]
    r = xs[0:A_WIDTH]
    k = xs[A_WIDTH:2 * A_WIDTH]
    v = xs[2 * A_WIDTH:3 * A_WIDTH]
    o0 = 3 * A_WIDTH
    w_lo = xs[o0:o0 + A_DECAY_LORA]
    a_lo = xs[o0 + A_DECAY_LORA:o0 + A_DECAY_LORA + A_AAA_LORA]
    g_lo = xs[o0 + A_DECAY_LORA + A_AAA_LORA:SHIFT_W]

    w0, a0, k_k, k_a, r_k = prm_ref[0], prm_ref[1], prm_ref[2], prm_ref[3], prm_ref[4]
    w = -jax.nn.softplus(-(w0 + _dot(wup_ref[...], jnp.tanh(w_lo)))) - 0.5
    log_decay = -jnp.exp(w)
    decay = log_decay if period is None else jnp.exp(log_decay)
    a = jax.nn.sigmoid(a0 + _dot(aup_ref[...], a_lo))
    g = _dot(gup_ref[...], jax.nn.sigmoid(g_lo))
    if has_vmix:
        mix = jax.nn.sigmoid(prm_ref[5] + _dot(v2_ref[...], _dot(v1_ref[...], v)))
        v = v + (vfirst_ref[0] - v) * mix
    kk = k * k_k
    ss = _head_sum(kk * kk, tb)
    kk = kk * _head_bcast(lax.rsqrt(jnp.maximum(ss, 1e-24)), tb)
    kx = k * (1.0 + (a - 1.0) * k_a)
    ops_ref[0] = r
    ops_ref[1] = kk
    ops_ref[2] = decay
    ops_ref[3] = kk * a
    ops_ref[4] = kx
    ops_ref[5] = v
    g_ref[...] = g
    bonus_ref[...] = _head_bcast(_head_sum(r * kx * r_k, tb), tb) * v


def _prep(pat, bnd, vfirst_ops, mu, prm, wup, aup, gup, v1, v2, *, n_seq, seq_len, tb):
    n = pat.shape[1]
    prompt = bnd is None
    has_vmix = vfirst_ops is not None
    if prompt:
        nt = seq_len // tb
        grid = (n_seq, nt)
        tok = lambda b, t: b * nt + t
        sem = ("parallel", "arbitrary")
    else:
        grid = (n // tb,)
        tok = lambda i: i
        sem = ("parallel",)
    cmap = lambda f: (lambda *g: f(tok(*g)))
    in_specs = [pl.BlockSpec((SHIFT_W, tb), cmap(lambda j: (0, j)))]
    args = [pat]
    if not prompt:
        in_specs.append(pl.BlockSpec((SHIFT_W, tb), cmap(lambda j: (0, j))))
        args.append(bnd)
    if has_vmix:
        in_specs.append(pl.BlockSpec((1, A_WIDTH, tb), cmap(lambda j: (5, 0, j))))
        args.append(vfirst_ops)
    const2 = lambda *g: (0, 0)
    const3 = lambda *g: (0, 0, 0)
    in_specs += [pl.BlockSpec((SHIFT_W, tb), const2),
                 pl.BlockSpec((6, A_WIDTH, tb), const3),
                 pl.BlockSpec((A_WIDTH, A_DECAY_LORA), const2),
                 pl.BlockSpec((A_WIDTH, A_AAA_LORA), const2),
                 pl.BlockSpec((A_WIDTH, A_GATE_LORA), const2),
                 pl.BlockSpec((A_MV_LORA, A_WIDTH), const2),
                 pl.BlockSpec((A_WIDTH, A_MV_LORA), const2)]
    args += [mu, prm, wup, aup, gup, v1, v2]
    out_specs = [pl.BlockSpec((6, A_WIDTH, tb), cmap(lambda j: (0, 0, j))),
                 pl.BlockSpec((A_WIDTH, tb), cmap(lambda j: (0, j))),
                 pl.BlockSpec((A_WIDTH, tb), cmap(lambda j: (0, j)))]
    out_shape = [jax.ShapeDtypeStruct((6, A_WIDTH, n), _F32),
                 jax.ShapeDtypeStruct((A_WIDTH, n), _F32),
                 jax.ShapeDtypeStruct((A_WIDTH, n), _F32)]
    scratch = [pltpu.VMEM((SHIFT_W, tb), _F32)] if prompt else []
    if prompt:
        out_specs.append(pl.BlockSpec((1, SHIFT_W, tb), lambda b, t: (b, 0, 0)))
        out_shape.append(jax.ShapeDtypeStruct((n_seq, SHIFT_W, tb), _F32))
    return pl.pallas_call(
        functools.partial(_prep_kernel, tb=tb, period=None if prompt else seq_len,
                          has_vmix=has_vmix),
        grid=grid, in_specs=in_specs, out_specs=out_specs, out_shape=out_shape,
        scratch_shapes=scratch, compiler_params=_cparams(*sem), name="rwkv_prep",
    )(*args)


_VH = A_HEAD_DIM // 2


_OP_R, _OP_KK, _OP_DECAY, _OP_B, _OP_K, _OP_V = range(6)


SCAN_ROWS = 128
_HD2 = A_HEAD_DIM * A_HEAD_DIM


def _scan_short_kernel(s_ref, z_ref, v_ref, o_ref, sout_ref, s_scr, *, n_steps):
    n_chunk = _HD2 // LANES
    for c in range(n_chunk):
        s_scr[pl.ds(c * LANES, LANES), :] = s_ref[0, :, c * LANES:(c + 1) * LANES].T

    def step(t, carry):
        def operand(op):
            return z_ref[0, t, op * A_HEAD_DIM:(op + 1) * A_HEAD_DIM, :]

        r, kk, decay, b, kx = (operand(op) for op in (_OP_R, _OP_KK, _OP_DECAY, _OP_B, _OP_K))
        for v in range(A_HEAD_DIM):
            rows = pl.ds(v * A_HEAD_DIM, A_HEAD_DIM)
            s = s_scr[rows, :]
            sa = -jnp.sum(s * kk, axis=0, keepdims=True)
            s_new = s * decay + sa * b + v_ref[0, t, pl.ds(v, 1), :] * kx
            s_scr[rows, :] = s_new
            o_ref[0, t, pl.ds(v, 1), :] = jnp.sum(s_new * r, axis=0, keepdims=True)
        return carry

    lax.fori_loop(0, n_steps, step, 0)
    for c in range(n_chunk):
        sout_ref[:, c * LANES:(c + 1) * LANES] = s_scr[pl.ds(c * LANES, LANES), :].T


def _scan_short(s_all, layer, z, v):
    n_grp, n_steps = z.shape[0], z.shape[1]
    return pl.pallas_call(
        functools.partial(_scan_short_kernel, n_steps=n_steps),
        grid=(n_grp,),
        in_specs=[pl.BlockSpec((1, SCAN_ROWS, _HD2), lambda g: (layer, g, 0)),
                  pl.BlockSpec((1, n_steps, 5 * A_HEAD_DIM, LANES), lambda g: (g, 0, 0, 0)),
                  pl.BlockSpec((1, n_steps, A_HEAD_DIM, LANES), lambda g: (g, 0, 0, 0))],
        out_specs=[pl.BlockSpec((1, n_steps, A_HEAD_DIM, LANES), lambda g: (g, 0, 0, 0)),
                   pl.BlockSpec((SCAN_ROWS, _HD2), lambda g: (g, 0))],
        out_shape=[jax.ShapeDtypeStruct((n_grp, n_steps, A_HEAD_DIM, LANES), _F32),
                   jax.ShapeDtypeStruct((n_grp * SCAN_ROWS, _HD2), _F32)],
        scratch_shapes=[pltpu.VMEM((_HD2, LANES), _F32)],
        compiler_params=_cparams("parallel"), name="rwkv_scan_short",
    )(s_all, z, v)


_SCAN_WIN = 32
_PHASE_OPS = (_OP_DECAY, _OP_R, _OP_KK, _OP_B, _OP_K, _OP_V)


def _scan_cm_kernel(*refs, tb):
    in_refs = refs[:SCAN_GROUP]
    o_ref, sout_ref, z_scr, zv_scr, o_scr, s_scr, d_scr = refs[SCAN_GROUP:]
    t_blk = pl.program_id(0)
    phase = pl.program_id(1)
    n_win = tb // _SCAN_WIN

    @pl.when(jnp.logical_and(t_blk == 0, phase == 0))
    def _():
        s_scr[...] = jnp.zeros_like(s_scr)

    def gather(k):
        rows = slice(k * A_HEADS, (k + 1) * A_HEADS)
        return jnp.concatenate([r[0, rows, :] for r in in_refs], axis=0)

    @pl.when(phase == 0)
    def _():
        kg = 8
        nrow = SCAN_GROUP * A_HEADS
        lane = lax.broadcasted_iota(jnp.int32, (kg * nrow, tb), 1) % _SCAN_WIN
        for k0 in range(0, A_HEAD_DIM, kg):
            logd = jnp.concatenate([gather(k0 + j) for j in range(kg)], axis=0)
            c = logd
            s = 1
            while s < _SCAN_WIN:
                c = c + jnp.where(lane >= s, pltpu.roll(c, s, 1), 0.0)
                s *= 2
            dm = jnp.exp(c)
            di = jnp.exp(-c)
            dp = jnp.exp(c - logd)
            for j in range(kg):
                rows = slice(j * nrow, (j + 1) * nrow)
                d_scr[k0 + j] = dm[rows]
                d_scr[A_HEAD_DIM + k0 + j] = di[rows]
                d_scr[2 * A_HEAD_DIM + k0 + j] = dp[rows]
                z_scr[k0 + j] = jnp.concatenate([dm[rows], dm[rows]], axis=0).T

    @pl.when(jnp.logical_and(phase >= 1, phase <= 4))
    def _():
        fac = jnp.where(phase == 1, 0, jnp.where(phase == 2, 2, 1)) * A_HEAD_DIM
        for k in range(A_HEAD_DIM):
            m = gather(k) * d_scr[fac + k]
            z_scr[phase * A_HEAD_DIM + k] = jnp.concatenate([m, m], axis=0).T

    @pl.when(phase == 5)
    def _():
        for v in range(_VH):
            lo = slice(v * A_HEADS, (v + 1) * A_HEADS)
            hi = slice((_VH + v) * A_HEADS, (_VH + v + 1) * A_HEADS)
            m = jnp.concatenate([r[0, lo, :] for r in in_refs]
                                + [r[0, hi, :] for r in in_refs], axis=0)
            zv_scr[pl.ds(v * tb, tb), :] = m.T

    @pl.when(phase == 6)
    def _():
        def row(t, slot, k):
            return z_scr[slot * A_HEAD_DIM + k, pl.ds(t, 1), :]

        zero = jnp.zeros((_VH, LANES), _F32)

        def window(w, carry):
            t0 = w * _SCAN_WIN
            acc = [zero, zero]
            for k in range(A_HEAD_DIM):
                acc[k % 2] = acc[k % 2] + s_scr[k] * row(t0, 2, k)

            def step(i, nacc):
                t = t0 + i
                tn = jnp.minimum(t + 1, tb - 1)
                sa = -nacc
                vv = zv_scr[pl.ds(t, _VH, stride=tb), :]
                out = [zero, zero]
                nxt = [zero, zero]
                for k in range(A_HEAD_DIM):
                    s_new = s_scr[k] + sa * row(t, 3, k) + vv * row(t, 4, k)
                    s_scr[k] = s_new
                    out[k % 2] = out[k % 2] + s_new * row(t, 1, k)
                    nxt[k % 2] = nxt[k % 2] + s_new * row(tn, 2, k)
                o_scr[pl.ds(pl.multiple_of(t * _VH, _VH), _VH), :] = out[0] + out[1]
                return nxt[0] + nxt[1]

            lax.fori_loop(0, _SCAN_WIN, step, acc[0] + acc[1])
            t_end = t0 + _SCAN_WIN - 1
            for k in range(A_HEAD_DIM):
                s_scr[k] = s_scr[k] * row(t_end, 0, k)
            return carry

        lax.fori_loop(0, n_win, window, 0)
        for v in range(_VH):
            mt = o_scr[pl.ds(v, tb, stride=_VH), :].T
            for half in range(2):
                for b in range(SCAN_GROUP):
                    src = (half * SCAN_GROUP + b) * A_HEADS
                    dst = (half * _VH + v) * A_HEADS
                    o_ref[b, dst:dst + A_HEADS, :] = mt[src:src + A_HEADS, :]
        sout_ref[...] = s_scr[...]


def _scan_cm(ops, seq_len, tb):
    nt = seq_len // tb
    n_phase = len(_PHASE_OPS) + 1

    def op_of_phase(p):
        op = _PHASE_OPS[-1]
        for i in range(len(_PHASE_OPS) - 2, -1, -1):
            op = jnp.where(p == i, _PHASE_OPS[i], op)
        return op

    def in_spec(b):
        return pl.BlockSpec((1, A_WIDTH, tb), lambda t, p: (op_of_phase(p), 0, b * nt + t))

    return pl.pallas_call(
        functools.partial(_scan_cm_kernel, tb=tb),
        grid=(nt, n_phase),
        in_specs=[in_spec(b) for b in range(SCAN_GROUP)],
        out_specs=[pl.BlockSpec((SCAN_GROUP, A_WIDTH, tb), lambda t, p: (0, 0, t)),
                   pl.BlockSpec((A_HEAD_DIM, _VH, LANES), lambda t, p: (0, 0, 0))],
        out_shape=[jax.ShapeDtypeStruct((SCAN_GROUP, A_WIDTH, seq_len), _F32),
                   jax.ShapeDtypeStruct((A_HEAD_DIM, _VH, LANES), _F32)],
        scratch_shapes=[pltpu.VMEM((5 * A_HEAD_DIM, tb, LANES), _F32),
                        pltpu.VMEM((_VH * tb, LANES), _F32),
                        pltpu.VMEM((tb * _VH, LANES), _F32),
                        pltpu.VMEM((A_HEAD_DIM, _VH, LANES), _F32),
                        pltpu.VMEM((3 * A_HEAD_DIM, SCAN_GROUP * A_HEADS, tb), _F32)],
        compiler_params=_cparams("arbitrary", "arbitrary"), name="rwkv_scan_cm",
    )(*([ops] * SCAN_GROUP))


def _gla_kernel(*refs, nb, chunk, valid, has_s0):
    it = iter(refs)
    p_ref = next(it)
    s0_ref = next(it) if has_s0 else None
    fup_ref = next(it)
    fb_ref = next(it)
    ng_ref = next(it)
    o_ref = next(it)
    sout_ref = next(it)
    s_scr = next(it)

    c_idx = pl.program_id(1)

    @pl.when(c_idx == 0)
    def _():
        if has_s0:
            s_scr[...] = s0_ref[0]
        else:
            s_scr[...] = jnp.zeros_like(s_scr)

    row = lax.broadcasted_iota(jnp.int32, (chunk, chunk), 0)
    col = lax.broadcasted_iota(jnp.int32, (chunk, chunk), 1)
    causal = row >= col
    eye_k = (lax.broadcasted_iota(jnp.int32, (B_DK, B_DK), 0)
             == lax.broadcasted_iota(jnp.int32, (B_DK, B_DK), 1))
    rows = nb * chunk
    trow = lax.broadcasted_iota(jnp.int32, (rows, B_KW), 0) % chunk
    ng = ng_ref[...]

    q_all = p_ref[:, :, 0:B_KW].reshape(rows, B_KW)
    k_all = p_ref[:, :, B_KW:2 * B_KW].reshape(rows, B_KW)
    f_lo = p_ref[:, :, 2 * B_KW + 2 * B_WIDTH:GLA_IN_W].reshape(rows, B_GATE_LORA)
    gk = jax.nn.log_sigmoid(_dot(f_lo, fup_ref[...]) + fb_ref[...]) / GLA_LOGIT_NORM
    if valid < chunk:
        gk = jnp.where(trow < valid, gk, 0.0)
        k_all = jnp.where(trow < valid, k_all, 0.0)
    bc = gk
    s = 1
    while s < chunk:
        bc = bc + jnp.where(trow >= s, pltpu.roll(bc, s, 0), 0.0)
        s *= 2
    b_last3 = bc.reshape(nb, chunk, B_KW)[:, chunk - 1:chunk, :]
    b_last = jnp.broadcast_to(b_last3, (nb, chunk, B_KW)).reshape(rows, B_KW)
    e_last3 = jnp.exp(b_last3)
    qt_all = q_all * (B_DK ** -0.5) * jnp.exp(bc)
    kt_all = k_all * jnp.exp(-bc)
    kd_all = k_all * jnp.exp(b_last - bc)

    units = [(b, h) for b in range(nb) for h in range(B_HEADS)]
    group = 16
    for u0 in range(0, len(units), group):
        grp = units[u0:u0 + group]
        att, qs, kv, vbs, sts = [], [], [], [], []
        for b, h in grp:
            rs = slice(b * chunk, (b + 1) * chunk)
            ks = slice(h * B_DK, (h + 1) * B_DK)
            vs = slice(2 * B_KW + h * B_DV, 2 * B_KW + (h + 1) * B_DV)
            qt = qt_all[rs, ks].astype(_BF16)
            kt = kt_all[rs, ks].astype(_BF16)
            kd = kd_all[rs, ks].astype(_BF16)
            vb = p_ref[b, :, vs].astype(_BF16)
            st = s_scr[b, h]
            att.append(lax.dot_general(qt, kt, (((1,), (1,)), ((), ())),
                                       preferred_element_type=_F32))
            qs.append(jnp.dot(qt, st.astype(_BF16), preferred_element_type=_F32))
            kv.append(lax.dot_general(kd, vb, (((0,), (0,)), ((), ())),
                                      preferred_element_type=_F32))
            vbs.append(vb)
            sts.append(st)
        outs = []
        for i in range(len(grp)):
            a = jnp.where(causal, att[i], 0.0).astype(_BF16)
            outs.append(jnp.dot(a, vbs[i], preferred_element_type=_F32) + qs[i])
        for i, (b, h) in enumerate(grp):
            ks = slice(h * B_DK, (h + 1) * B_DK)
            gs = slice(2 * B_KW + B_WIDTH + h * B_DV, 2 * B_KW + B_WIDTH + (h + 1) * B_DV)
            e_col = jnp.sum(jnp.where(eye_k, jnp.broadcast_to(e_last3[b, :, ks], (B_DK, B_DK)), 0.0),
                            axis=1, keepdims=True)
            s_scr[b, h] = sts[i] * e_col + kv[i]
            o = outs[i]
            gh = p_ref[b, :, gs]
            on = o * lax.rsqrt(jnp.mean(o * o, -1, keepdims=True) + RMS_EPS) * ng
            o_ref[b, :, h * B_DV:(h + 1) * B_DV] = on * (gh * jax.nn.sigmoid(gh))

    @pl.when(c_idx == pl.num_programs(1) - 1)
    def _():
        sout_ref[...] = s_scr[...]


def _gla(pb3, s0_all, layer, fup, fb, ng, *, nb, chunk, valid):
    n_seq, seq_len = pb3.shape[0], pb3.shape[1]
    has_s0 = s0_all is not None
    in_specs = [pl.BlockSpec((nb, chunk, GLA_IN_PAD), lambda i, c: (i, c, 0))]
    args = [pb3]
    if has_s0:
        in_specs.append(pl.BlockSpec((1, nb, B_HEADS, B_DK, B_DV),
                                     lambda i, c: (layer, i, 0, 0, 0)))
        args.append(s0_all)
    in_specs += [pl.BlockSpec((B_GATE_LORA, B_KW), lambda i, c: (0, 0)),
                 pl.BlockSpec((1, B_KW), lambda i, c: (0, 0)),
                 pl.BlockSpec((1, B_DV), lambda i, c: (0, 0))]
    args += [fup, fb, ng]
    return pl.pallas_call(
        functools.partial(_gla_kernel, nb=nb, chunk=chunk, valid=valid, has_s0=has_s0),
        grid=(n_seq // nb, seq_len // chunk),
        in_specs=in_specs,
        out_specs=[pl.BlockSpec((nb, chunk, B_WIDTH), lambda i, c: (i, c, 0)),
                   pl.BlockSpec((nb, B_HEADS, B_DK, B_DV), lambda i, c: (i, 0, 0, 0))],
        out_shape=[jax.ShapeDtypeStruct((n_seq, seq_len, B_WIDTH), _F32),
                   jax.ShapeDtypeStruct((n_seq, B_HEADS, B_DK, B_DV), _F32)],
        scratch_shapes=[pltpu.VMEM((nb, B_HEADS, B_DK, B_DV), _F32)],
        compiler_params=_cparams("parallel", "arbitrary"), name="gla",
    )(*args)


def _outproj_kernel(x_ref, o_ref, gate_ref, bonus_ref, lng_ref, lnb_ref, ob_ref, wa_ref, wb_ref,
                    g_ref, b_ref, y_ref, *, tm):
    o3 = o_ref[...].reshape(A_HEAD_DIM, A_HEADS, tm)
    m = jnp.mean(o3, axis=0, keepdims=True)
    oc = o3 - m
    var = jnp.mean(oc * oc, axis=0, keepdims=True)
    on = (oc * lax.rsqrt(var + GN_EPS_A)).reshape(A_WIDTH, tm)
    oa = ((on * lng_ref[...] + lnb_ref[...] + bonus_ref[...]) * gate_ref[...]).T
    mix = _dot(oa, wa_ref[...]) + _dot(ob_ref[...], wb_ref[...])
    y_ref[...] = _layer_norm(ALPHA * x_ref[...] + mix, g_ref[...], b_ref[...])


def _outproj(x, o_cm, gate, bonus, lng, lnb, ob, wa, wb, g, b, tm):
    n = x.shape[0]
    rowb = lambda w: pl.BlockSpec((tm, w), lambda i: (i, 0))
    const = lambda r, c: pl.BlockSpec((r, c), lambda i: (0, 0))
    cm = pl.BlockSpec((A_WIDTH, tm), lambda i: (0, i))
    if o_cm.ndim == 3:
        nt = o_cm.shape[2] // tm
        o_spec = pl.BlockSpec((1, A_WIDTH, tm), lambda i: (i // nt, 0, i % nt))
    else:
        o_spec = cm
    return pl.pallas_call(
        functools.partial(_outproj_kernel, tm=tm),
        grid=(n // tm,),
        in_specs=[rowb(D_MODEL), o_spec, cm, cm, const(A_WIDTH, tm), const(A_WIDTH, tm),
                  rowb(B_WIDTH), const(A_WIDTH, D_MODEL), const(B_WIDTH, D_MODEL),
                  const(1, D_MODEL), const(1, D_MODEL)],
        out_specs=rowb(D_MODEL),
        out_shape=jax.ShapeDtypeStruct((n, D_MODEL), _F32),
        compiler_params=_cparams("parallel"), name="outproj_ln",
    )(x, o_cm, gate, bonus, lng, lnb, ob, wa, wb, g, b)


def _ffn_kernel(*refs, tm, period):
    it = iter(refs)
    x_ref = next(it)
    p1_ref = next(it) if period is not None else None
    p2_ref = next(it) if period is not None else None
    wup_ref = next(it)
    cw_ref = next(it)
    cb_ref = next(it)
    wdn_ref = next(it)
    g_ref = next(it)
    b_ref = next(it)
    y_ref = next(it)
    gate_ref = next(it)
    carry_ref = next(it) if period is None else None

    x = x_ref[...]
    u = jnp.dot(x.astype(_BF16), wup_ref[...], preferred_element_type=_F32)
    gate = u[:, :D_FF]
    val = u[:, D_FF:]
    row = lax.broadcasted_iota(jnp.int32, (tm, D_FF), 0)
    r1 = pltpu.roll(gate, 1, 0)
    r2 = pltpu.roll(gate, 2, 0)
    if period is None:
        @pl.when(pl.program_id(1) == 0)
        def _():
            carry_ref[...] = jnp.zeros_like(carry_ref)
        c = carry_ref[...]
        row8 = lax.broadcasted_iota(jnp.int32, (SUBLANES, D_FF), 0)
        top1 = jnp.where(row8 < 1, pltpu.roll(c, 1, 0), r1[:SUBLANES])
        top2 = jnp.where(row8 < 2, pltpu.roll(c, 2, 0), r2[:SUBLANES])
        g1 = jnp.concatenate([top1, r1[SUBLANES:]], axis=0)
        g2 = jnp.concatenate([top2, r2[SUBLANES:]], axis=0)
        carry_ref[...] = gate[tm - SUBLANES:]
        gate_ref[0] = gate[tm - SUBLANES:]
    else:
        g1 = jnp.where(row % period >= 1, r1, p1_ref[...])
        g2 = jnp.where(row % period >= 2, r2, p2_ref[...])
        gate_ref[...] = gate
    acc = cb_ref[...] + g2 * cw_ref[0:1, :] + g1 * cw_ref[1:2, :] + gate * cw_ref[2:3, :]
    h = 0.5 * acc * (1.0 + lax.erf(acc * (2.0 ** -0.5))) * val
    f = jnp.dot(h.astype(_BF16), wdn_ref[...], preferred_element_type=_F32)
    y_ref[...] = _layer_norm(ALPHA * x + f, g_ref[...], b_ref[...])


def _ffn(x, p1, p2, wup, cw, cb, wdn, g, b, *, n_seq, seq_len, tm):
    n = x.shape[0]
    prompt = p1 is None
    if prompt:
        nt = seq_len // tm
        grid = (n_seq, nt)
        rmap = lambda bb, t: (bb * nt + t, 0)
        sem = ("parallel", "arbitrary")
    else:
        grid = (n // tm,)
        rmap = lambda i: (i, 0)
        sem = ("parallel",)
    const = lambda r, c: pl.BlockSpec((r, c), lambda *gidx: (0, 0), pipeline_mode=pl.Buffered(1))
    in_specs = [pl.BlockSpec((tm, D_MODEL), rmap)]
    args = [x]
    if not prompt:
        in_specs += [pl.BlockSpec((tm, D_FF), rmap), pl.BlockSpec((tm, D_FF), rmap)]
        args += [p1, p2]
    in_specs += [const(D_MODEL, 2 * D_FF), const(CONV_W, D_FF), const(1, D_FF),
                 const(D_FF, D_MODEL), const(1, D_MODEL), const(1, D_MODEL)]
    args += [wup, cw, cb, wdn, g, b]
    if prompt:
        gate_spec = pl.BlockSpec((1, SUBLANES, D_FF), lambda bb, t: (bb, 0, 0))
        gate_shape = jax.ShapeDtypeStruct((n_seq, SUBLANES, D_FF), _F32)
        scratch = [pltpu.VMEM((SUBLANES, D_FF), _F32)]
    else:
        gate_spec = pl.BlockSpec((tm, D_FF), rmap)
        gate_shape = jax.ShapeDtypeStruct((n, D_FF), _F32)
        scratch = []
    return pl.pallas_call(
        functools.partial(_ffn_kernel, tm=tm, period=None if prompt else seq_len),
        grid=grid, in_specs=in_specs,
        out_specs=[pl.BlockSpec((tm, D_MODEL), rmap), gate_spec],
        out_shape=[jax.ShapeDtypeStruct((n, D_MODEL), _F32), gate_shape],
        scratch_shapes=scratch, compiler_params=_cparams(*sem), name="ffn_ln",
    )(*args)


def _ops_to_short(ops, n_grp, seq_len):
    n_b = SCAN_ROWS // A_HEADS
    x = ops.reshape(6, A_HEAD_DIM, A_HEADS, n_grp, n_b, seq_len)
    z = jnp.transpose(x[:_OP_V], (3, 5, 0, 1, 4, 2)).reshape(n_grp, seq_len, 5 * A_HEAD_DIM, LANES)
    v = jnp.transpose(x[_OP_V], (2, 4, 0, 3, 1)).reshape(n_grp, seq_len, A_HEAD_DIM, LANES)
    return z, v


def _short_to_cm(o, n_grp, seq_len):
    n_b = SCAN_ROWS // A_HEADS
    x = o.reshape(n_grp, seq_len, A_HEAD_DIM, n_b, A_HEADS)
    return jnp.transpose(x, (2, 4, 0, 3, 1)).reshape(A_WIDTH, n_grp * n_b * seq_len)


def _lanes_to_state(s):
    x = s.reshape(A_HEAD_DIM, _VH, 2, SCAN_GROUP, A_HEADS)
    return jnp.transpose(x, (3, 4, 2, 1, 0)).reshape(SCAN_GROUP, A_HEADS, A_HEAD_DIM, A_HEAD_DIM)


def _kh_rows(x):
    return jnp.swapaxes(x.reshape((A_HEADS, A_HEAD_DIM) + x.shape[1:]), 0, 1).reshape(x.shape)


def _rwkv_rows(x):
    blocks = [_kh_rows(x[i * A_WIDTH:(i + 1) * A_WIDTH]) for i in range(3)]
    return jnp.concatenate(blocks + [x[3 * A_WIDTH:]], axis=0)


def _rwkv_rows_inv(x):
    def hk(y):
        return jnp.swapaxes(y.reshape((A_HEAD_DIM, A_HEADS) + y.shape[1:]), 0, 1).reshape(y.shape)
    blocks = [hk(x[i * A_WIDTH:(i + 1) * A_WIDTH]) for i in range(3)]
    return jnp.concatenate(blocks + [x[3 * A_WIDTH:]], axis=0)


def _bcast_cols(p, tb):
    return jnp.broadcast_to(p[:, None], (p.shape[0], tb))


def _layer_weights(P, l, tb_prep, tb_post):
    w_in = P['w_in'][l]
    wta = _rwkv_rows(w_in[:, :SHIFT_W].T).astype(_BF16)
    wb = jnp.pad(w_in[:, SHIFT_W:], ((0, 0), (0, GLA_IN_PAD - GLA_IN_W))).astype(_BF16)
    vb = _kh_rows(P['vres_bias'][l - 1]) if l > 0 else jnp.zeros((A_WIDTH,), _F32)
    prm = jnp.stack([_kh_rows(P['w0'][l]), _kh_rows(P['a0'][l]), _kh_rows(P['k_k'][l]),
                     _kh_rows(P['k_a'][l]), _kh_rows(P['r_k'][l].reshape(A_WIDTH)), vb])
    prm = jnp.broadcast_to(prm[:, :, None], (6, A_WIDTH, tb_prep))
    if l > 0:
        v1 = _kh_rows(P['vres_down'][l - 1]).T.astype(_BF16)
        v2 = _kh_rows(P['vres_up'][l - 1].T).astype(_BF16)
    else:
        v1 = jnp.zeros((A_MV_LORA, A_WIDTH), _BF16)
        v2 = jnp.zeros((A_WIDTH, A_MV_LORA), _BF16)
    return dict(
        wta=wta, wb=wb,
        mu=_bcast_cols(_rwkv_rows(P['tok_mu'][l]), tb_prep), prm=prm,
        wup=_kh_rows(P['w_lora_up'][l].T).astype(_BF16),
        aup=_kh_rows(P['a_lora_up'][l].T).astype(_BF16),
        gup=_kh_rows(P['g_lora_up'][l].T).astype(_BF16),
        v1=v1, v2=v2,
        lng=_bcast_cols(_kh_rows(P['lnx_g'][l]), tb_post),
        lnb=_bcast_cols(_kh_rows(P['lnx_b'][l]), tb_post),
        fup=P['gla_f_up'][l].astype(_BF16), fb=P['gla_f_bias'][l][None], ng=P['gla_norm_g'][l][None],
        woa=_kh_rows(P['w_out'][l][:A_WIDTH]).astype(_BF16), wob=P['w_out'][l][A_WIDTH:].astype(_BF16),
        ln1g=P['ln1_g'][l][None], ln1b=P['ln1_b'][l][None],
        w_up=P['w_up'][l].astype(_BF16), cw=P['conv_w'][l], cb=P['conv_b'][l][None],
        w_down=P['w_down'][l].astype(_BF16), ln2g=P['ln2_g'][l][None], ln2b=P['ln2_b'][l][None])


def _trunk(x3, states, P, prompt):
    n_seq, seq_len, _ = x3.shape
    n = n_seq * seq_len
    x = x3.reshape(n, D_MODEL)
    tb = 2 * LANES
    if prompt:
        tt, scan_tb, gla_chunk, gla_valid, gla_t, tm_out, tm_ffn = 512, LANES, GLA_CHUNK, GLA_CHUNK, seq_len, 256, 256
        st_gla = None
    else:
        st_rwkv, st_shift, st_gla, st_conv = states
        tt, scan_tb, gla_chunk, gla_valid, gla_t, tm_out, tm_ffn = 512, seq_len, SUBLANES, seq_len, SUBLANES, 256, 256
        n_grp = n_seq * A_HEADS // SCAN_ROWS
        s_all = st_rwkv.reshape(DEPTH, n_seq * A_HEADS, _HD2)
    new_rwkv, new_shift, new_gla, new_conv = [], [], [], []
    ops0 = None
    for l in range(DEPTH):
        W = _layer_weights(P, l, tb, tm_out)
        pat, pb = _proj(x, W['wta'], W['wb'], tt)
        if prompt:
            bnd = None
        else:
            bnd = jnp.repeat(_rwkv_rows(st_shift[l][:, 0, :].T), seq_len, axis=1)
        prep_out = _prep(pat, bnd, ops0, W['mu'], W['prm'], W['wup'], W['aup'], W['gup'],
                         W['v1'], W['v2'], n_seq=n_seq, seq_len=seq_len, tb=tb)
        ops, g, bonus = prep_out[:3]
        if l == 0:
            ops0 = ops
        if prompt:
            o_cm, s_fin = _scan_cm(ops, seq_len, scan_tb)
            s_fin = _lanes_to_state(s_fin)
            last_cols = prep_out[3][:, :, tb - 1].T
        else:
            z, zv = _ops_to_short(ops, n_grp, seq_len)
            o, s_fin = _scan_short(s_all, l, z, zv)
            o_cm = _short_to_cm(o, n_grp, seq_len)
            s_fin = s_fin.reshape(n_seq, A_HEADS, A_HEAD_DIM, A_HEAD_DIM)
            last_cols = pat[:, seq_len - 1::seq_len]
        pb3 = pb.reshape(n_seq, seq_len, GLA_IN_PAD)
        if gla_t != seq_len:
            pb3 = jnp.pad(pb3, ((0, 0), (0, gla_t - seq_len), (0, 0)))
        ob, s_gla = _gla(pb3, st_gla, l, W['fup'], W['fb'], W['ng'],
                         nb=SCAN_GROUP, chunk=gla_chunk, valid=gla_valid)
        ob = ob[:, :seq_len].reshape(n, B_WIDTH)
        x1 = _outproj(x, o_cm, g, bonus, W['lng'], W['lnb'], ob, W['woa'], W['wob'],
                      W['ln1g'], W['ln1b'], tm_out)
        if prompt:
            p1 = p2 = None
        else:
            past = st_conv[l]
            zero = jnp.zeros((n_seq, 1, D_FF), _F32)
            p1 = jnp.concatenate([past[:, 1:2], zero, zero, zero], axis=1).reshape(n, D_FF)
            p2 = jnp.concatenate([past[:, 0:1], past[:, 1:2], zero, zero], axis=1).reshape(n, D_FF)
        x, gate = _ffn(x1, p1, p2, W['w_up'], W['cw'], W['cb'], W['w_down'], W['ln2g'], W['ln2b'],
                       n_seq=n_seq, seq_len=seq_len, tm=tm_ffn)
        new_rwkv.append(s_fin)
        new_shift.append(_rwkv_rows_inv(last_cols).T[:, None, :])
        new_gla.append(s_gla)
        if prompt:
            new_conv.append(gate[:, SUBLANES - (CONV_W - 1):])
        else:
            new_conv.append(gate.reshape(n_seq, seq_len, D_FF)[:, seq_len - (CONV_W - 1):])
    return (x.reshape(n_seq, seq_len, D_MODEL), jnp.stack(new_rwkv), jnp.stack(new_shift),
            jnp.stack(new_gla), jnp.stack(new_conv))


def kernel(x_prompt, x_sample, state_rwkv, state_shift, state_gla, state_conv, w_in, tok_mu, w0, w_lora_up, a0, a_lora_up, g_lora_up, k_k, k_a, r_k, lnx_g, lnx_b, vres_bias, vres_down, vres_up, gla_f_up, gla_f_bias, gla_norm_g, w_out, ln1_g, ln1_b, w_up, conv_w, conv_b, w_down, ln2_g, ln2_b):
    P = dict(w_in=w_in, tok_mu=tok_mu, w0=w0, w_lora_up=w_lora_up, a0=a0, a_lora_up=a_lora_up,
             g_lora_up=g_lora_up, k_k=k_k, k_a=k_a, r_k=r_k, lnx_g=lnx_g, lnx_b=lnx_b,
             vres_bias=vres_bias, vres_down=vres_down, vres_up=vres_up, gla_f_up=gla_f_up,
             gla_f_bias=gla_f_bias, gla_norm_g=gla_norm_g, w_out=w_out, ln1_g=ln1_g, ln1_b=ln1_b,
             w_up=w_up, conv_w=conv_w, conv_b=conv_b, w_down=w_down, ln2_g=ln2_g, ln2_b=ln2_b)
    y_p, rwkv_p, shift_p, gla_p, conv_p = _trunk(x_prompt, None, P, True)
    y_s, rwkv_s, shift_s, gla_s, conv_s = _trunk(
        x_sample, (state_rwkv, state_shift, state_gla, state_conv), P, False)
    return (y_p, y_s, rwkv_p, rwkv_s, shift_p, shift_s, gla_p, gla_s, conv_p, conv_s)
```

```python
import functools

import jax
import jax.numpy as jnp
from jax import lax
from jax.experimental import pallas as pl
from jax.experimental.pallas import tpu as pltpu

D_MODEL = 1024
DEPTH = 4
A_WIDTH = 512
A_HEAD_DIM = 64
A_HEADS = 8
A_DECAY_LORA = 64
A_AAA_LORA = 64
A_MV_LORA = 32
A_GATE_LORA = 128
GN_EPS_A = 64e-5
B_WIDTH = 512
B_HEADS = 4
B_DV = 128
B_DK = 64
B_KW = 256
B_GATE_LORA = 16
GLA_LOGIT_NORM = 16.0
GLA_CHUNK = 64
RMS_EPS = 1e-5
D_FF = 2816
CONV_W = 3
ALPHA = (2 * DEPTH) ** 0.25
LN_EPS = 1e-5
SHIFT_W = 3 * A_WIDTH + A_DECAY_LORA + A_AAA_LORA + A_GATE_LORA
GLA_IN_W = 2 * B_KW + 2 * B_WIDTH + B_GATE_LORA
GLA_IN_PAD = 1664

LANES = 128
SUBLANES = 8
SCAN_GROUP = 8
VMEM_LIMIT = 56 * 1024 * 1024


_F32 = jnp.float32
_BF16 = jnp.bfloat16


def _cparams(*sem):
    return pltpu.CompilerParams(dimension_semantics=sem, vmem_limit_bytes=VMEM_LIMIT)


def _dot(a, b):
    return jnp.dot(a.astype(_BF16), b.astype(_BF16), preferred_element_type=_F32)


def _layer_norm(y, g, b):
    mu = jnp.mean(y, -1, keepdims=True)
    yc = y - mu
    var = jnp.mean(yc * yc, -1, keepdims=True)
    return yc * lax.rsqrt(var + LN_EPS) * g + b


def _head_sum(x, tb):
    return jnp.sum(x.reshape(A_HEAD_DIM, A_HEADS, tb), axis=0)


def _head_bcast(s, tb):
    return jnp.broadcast_to(s[None], (A_HEAD_DIM, A_HEADS, tb)).reshape(A_WIDTH, tb)


def _prep_kernel(*refs, tb, period, has_vmix):
    it = iter(refs)
    x_ref = next(it)
    wta_ref = next(it)
    wb_ref = next(it)
    bnd_ref = next(it) if period is not None else None
    vfirst_ref = next(it) if has_vmix else None
    mu_ref = next(it)
    prm_ref = next(it)
    wup_ref = next(it)
    aup_ref = next(it)
    gup_ref = next(it)
    v1_ref = next(it)
    v2_ref = next(it)
    ops_ref = next(it)
    g_ref = next(it)
    bonus_ref = next(it)
    pb_ref = next(it)
    pa_out_ref = next(it)
    carry_ref = next(it) if period is None else None

    x = x_ref[...].astype(_BF16)
    pb_ref[...] = jnp.dot(x, wb_ref[0], preferred_element_type=_F32)
    pa = lax.dot_general(wta_ref[0], x, (((1,), (1,)), ((), ())), preferred_element_type=_F32)
    lane = lax.broadcasted_iota(jnp.int32, (SHIFT_W, tb), 1)
    rolled = pltpu.roll(pa, 1, 1)
    if period is None:
        @pl.when(pl.program_id(1) == 0)
        def _():
            carry_ref[...] = jnp.zeros_like(carry_ref)
        prev = jnp.where(lane == 0, pltpu.roll(carry_ref[...], 1, 1), rolled)
        carry_ref[...] = pa
        pa_out_ref[0] = pa
    else:
        prev = jnp.where(lane % period == 0, bnd_ref[...], rolled)
        pa_out_ref[...] = pa

    xs = pa + (prev - pa) * mu_ref[...]
    r = xs[0:A_WIDTH]
    k = xs[A_WIDTH:2 * A_WIDTH]
    v = xs[2 * A_WIDTH:3 * A_WIDTH]
    o0 = 3 * A_WIDTH
    w_lo = xs[o0:o0 + A_DECAY_LORA]
    a_lo = xs[o0 + A_DECAY_LORA:o0 + A_DECAY_LORA + A_AAA_LORA]
    g_lo = xs[o0 + A_DECAY_LORA + A_AAA_LORA:SHIFT_W]

    w0, a0, k_k, k_a, r_k = prm_ref[0], prm_ref[1], prm_ref[2], prm_ref[3], prm_ref[4]
    w = -jax.nn.softplus(-(w0 + _dot(wup_ref[...], jnp.tanh(w_lo)))) - 0.5
    log_decay = -jnp.exp(w)
    decay = log_decay if period is None else jnp.exp(log_decay)
    a = jax.nn.sigmoid(a0 + _dot(aup_ref[...], a_lo))
    g = _dot(gup_ref[...], jax.nn.sigmoid(g_lo))
    if has_vmix:
        mix = jax.nn.sigmoid(prm_ref[5] + _dot(v2_ref[...], _dot(v1_ref[...], v)))
        v = v + (vfirst_ref[0] - v) * mix
    kk = k * k_k
    ss = _head_sum(kk * kk, tb)
    kk = kk * _head_bcast(lax.rsqrt(jnp.maximum(ss, 1e-24)), tb)
    kx = k * (1.0 + (a - 1.0) * k_a)
    ops_ref[0] = r
    ops_ref[1] = kk
    ops_ref[2] = decay
    ops_ref[3] = kk * a
    ops_ref[4] = kx
    ops_ref[5] = v
    g_ref[...] = g
    bonus_ref[...] = _head_bcast(_head_sum(r * kx * r_k, tb), tb) * v


def _prep(x, wta_all, wb_all, layer, bnd, vfirst_ops, mu, prm, wup, aup, gup, v1, v2, *,
          n_seq, seq_len, tb):
    n = x.shape[0]
    prompt = bnd is None
    has_vmix = vfirst_ops is not None
    if prompt:
        nt = seq_len // tb
        grid = (n_seq, nt)
        tok = lambda b, t: b * nt + t
        sem = ("parallel", "arbitrary")
    else:
        grid = (n // tb,)
        tok = lambda i: i
        sem = ("parallel",)
    cmap = lambda f: (lambda *g: f(tok(*g)))
    in_specs = [pl.BlockSpec((tb, D_MODEL), cmap(lambda j: (j, 0))),
                pl.BlockSpec((1, SHIFT_W, D_MODEL), lambda *g: (layer, 0, 0)),
                pl.BlockSpec((1, D_MODEL, GLA_IN_PAD), lambda *g: (layer, 0, 0))]
    args = [x, wta_all, wb_all]
    if not prompt:
        in_specs.append(pl.BlockSpec((SHIFT_W, tb), cmap(lambda j: (0, j))))
        args.append(bnd)
    if has_vmix:
        in_specs.append(pl.BlockSpec((1, A_WIDTH, tb), cmap(lambda j: (5, 0, j))))
        args.append(vfirst_ops)
    const2 = lambda *g: (0, 0)
    const3 = lambda *g: (0, 0, 0)
    in_specs += [pl.BlockSpec((SHIFT_W, tb), const2),
                 pl.BlockSpec((6, A_WIDTH, tb), const3),
                 pl.BlockSpec((A_WIDTH, A_DECAY_LORA), const2),
                 pl.BlockSpec((A_WIDTH, A_AAA_LORA), const2),
                 pl.BlockSpec((A_WIDTH, A_GATE_LORA), const2),
                 pl.BlockSpec((A_MV_LORA, A_WIDTH), const2),
                 pl.BlockSpec((A_WIDTH, A_MV_LORA), const2)]
    args += [mu, prm, wup, aup, gup, v1, v2]
    out_specs = [pl.BlockSpec((6, A_WIDTH, tb), cmap(lambda j: (0, 0, j))),
                 pl.BlockSpec((A_WIDTH, tb), cmap(lambda j: (0, j))),
                 pl.BlockSpec((A_WIDTH, tb), cmap(lambda j: (0, j))),
                 pl.BlockSpec((tb, GLA_IN_PAD), cmap(lambda j: (j, 0)))]
    out_shape = [jax.ShapeDtypeStruct((6, A_WIDTH, n), _F32),
                 jax.ShapeDtypeStruct((A_WIDTH, n), _F32),
                 jax.ShapeDtypeStruct((A_WIDTH, n), _F32),
                 jax.ShapeDtypeStruct((n, GLA_IN_PAD), _F32)]
    scratch = [pltpu.VMEM((SHIFT_W, tb), _F32)] if prompt else []
    if prompt:
        out_specs.append(pl.BlockSpec((1, SHIFT_W, tb), lambda b, t: (b, 0, 0)))
        out_shape.append(jax.ShapeDtypeStruct((n_seq, SHIFT_W, tb), _F32))
    else:
        out_specs.append(pl.BlockSpec((SHIFT_W, tb), cmap(lambda j: (0, j))))
        out_shape.append(jax.ShapeDtypeStruct((SHIFT_W, n), _F32))
    return pl.pallas_call(
        functools.partial(_prep_kernel, tb=tb, period=None if prompt else seq_len,
                          has_vmix=has_vmix),
        grid=grid, in_specs=in_specs, out_specs=out_specs, out_shape=out_shape,
        scratch_shapes=scratch, compiler_params=_cparams(*sem), name="rwkv_prep",
    )(*args)


_VH = A_HEAD_DIM // 2


_OP_R, _OP_KK, _OP_DECAY, _OP_B, _OP_K, _OP_V = range(6)


SCAN_ROWS = 128
_HD2 = A_HEAD_DIM * A_HEAD_DIM


def _scan_short_kernel(s_ref, z_ref, v_ref, o_ref, sout_ref, s_scr, *, n_steps):
    n_b = SCAN_ROWS // A_HEADS
    for v in range(0, A_HEAD_DIM, 2):
        pair = [s_ref[0, :, :, v + j, :].reshape(SCAN_ROWS, A_HEAD_DIM) for j in range(2)]
        s_scr[pl.ds(v * A_HEAD_DIM, LANES), :] = jnp.concatenate(pair, axis=1).T

    def step(t, carry):
        def operand(op):
            return z_ref[0, t, op * A_HEAD_DIM:(op + 1) * A_HEAD_DIM, :]

        r, kk, decay, b, kx = (operand(op) for op in (_OP_R, _OP_KK, _OP_DECAY, _OP_B, _OP_K))
        for v in range(A_HEAD_DIM):
            rows = pl.ds(v * A_HEAD_DIM, A_HEAD_DIM)
            s = s_scr[rows, :]
            sa = -jnp.sum(s * kk, axis=0, keepdims=True)
            s_new = s * decay + sa * b + v_ref[0, t, pl.ds(v, 1), :] * kx
            s_scr[rows, :] = s_new
            o_ref[0, t, pl.ds(v, 1), :] = jnp.sum(s_new * r, axis=0, keepdims=True)
        return carry

    lax.fori_loop(0, n_steps, step, 0)
    for v in range(0, A_HEAD_DIM, 2):
        pair = s_scr[pl.ds(v * A_HEAD_DIM, LANES), :].T
        for j in range(2):
            sout_ref[:, :, v + j, :] = pair[:, j * A_HEAD_DIM:(j + 1) * A_HEAD_DIM].reshape(
                n_b, A_HEADS, A_HEAD_DIM)


def _scan_short(s_all, layer, z, v):
    n_grp, n_steps = z.shape[0], z.shape[1]
    n_b = SCAN_ROWS // A_HEADS
    s_block = (n_b, A_HEADS, A_HEAD_DIM, A_HEAD_DIM)
    return pl.pallas_call(
        functools.partial(_scan_short_kernel, n_steps=n_steps),
        grid=(n_grp,),
        in_specs=[pl.BlockSpec((1,) + s_block, lambda g: (layer, g, 0, 0, 0)),
                  pl.BlockSpec((1, n_steps, 5 * A_HEAD_DIM, LANES), lambda g: (g, 0, 0, 0)),
                  pl.BlockSpec((1, n_steps, A_HEAD_DIM, LANES), lambda g: (g, 0, 0, 0))],
        out_specs=[pl.BlockSpec((1, n_steps, A_HEAD_DIM, LANES), lambda g: (g, 0, 0, 0)),
                   pl.BlockSpec(s_block, lambda g: (g, 0, 0, 0))],
        out_shape=[jax.ShapeDtypeStruct((n_grp, n_steps, A_HEAD_DIM, LANES), _F32),
                   jax.ShapeDtypeStruct((n_grp * n_b,) + s_block[1:], _F32)],
        scratch_shapes=[pltpu.VMEM((_HD2, LANES), _F32)],
        compiler_params=_cparams("parallel"), name="rwkv_scan_short",
    )(s_all, z, v)


_SCAN_WIN = 32
_PHASE_OPS = (_OP_DECAY, _OP_R, _OP_KK, _OP_B, _OP_K, _OP_V)


def _scan_cm_kernel(*refs, tb):
    in_refs = refs[:SCAN_GROUP]
    o_ref, sout_ref, z_scr, zv_scr, o_scr, s_scr, d_scr = refs[SCAN_GROUP:]
    t_blk = pl.program_id(0)
    phase = pl.program_id(1)
    n_win = tb // _SCAN_WIN

    @pl.when(jnp.logical_and(t_blk == 0, phase == 0))
    def _():
        s_scr[...] = jnp.zeros_like(s_scr)

    def gather(k):
        rows = slice(k * A_HEADS, (k + 1) * A_HEADS)
        return jnp.concatenate([r[0, rows, :] for r in in_refs], axis=0)

    @pl.when(phase == 0)
    def _():
        kg = 8
        nrow = SCAN_GROUP * A_HEADS
        lane = lax.broadcasted_iota(jnp.int32, (kg * nrow, tb), 1) % _SCAN_WIN
        for k0 in range(0, A_HEAD_DIM, kg):
            logd = jnp.concatenate([gather(k0 + j) for j in range(kg)], axis=0)
            c = logd
            s = 1
            while s < _SCAN_WIN:
                c = c + jnp.where(lane >= s, pltpu.roll(c, s, 1), 0.0)
                s *= 2
            dm = jnp.exp(c)
            di = jnp.exp(-c)
            dp = jnp.exp(c - logd)
            for j in range(kg):
                rows = slice(j * nrow, (j + 1) * nrow)
                d_scr[k0 + j] = dm[rows]
                d_scr[A_HEAD_DIM + k0 + j] = di[rows]
                d_scr[2 * A_HEAD_DIM + k0 + j] = dp[rows]
                z_scr[k0 + j] = jnp.concatenate([dm[rows], dm[rows]], axis=0).T

    @pl.when(jnp.logical_and(phase >= 1, phase <= 4))
    def _():
        fac = jnp.where(phase == 1, 0, jnp.where(phase == 2, 2, 1)) * A_HEAD_DIM
        for k in range(A_HEAD_DIM):
            m = gather(k) * d_scr[fac + k]
            z_scr[phase * A_HEAD_DIM + k] = jnp.concatenate([m, m], axis=0).T

    @pl.when(phase == 5)
    def _():
        for v in range(_VH):
            lo = slice(v * A_HEADS, (v + 1) * A_HEADS)
            hi = slice((_VH + v) * A_HEADS, (_VH + v + 1) * A_HEADS)
            m = jnp.concatenate([r[0, lo, :] for r in in_refs]
                                + [r[0, hi, :] for r in in_refs], axis=0)
            zv_scr[pl.ds(v * tb, tb), :] = m.T

    @pl.when(phase == 6)
    def _():
        def row(t, slot, k):
            return z_scr[slot * A_HEAD_DIM + k, pl.ds(t, 1), :]

        zero = jnp.zeros((_VH, LANES), _F32)

        def window(w, carry):
            t0 = w * _SCAN_WIN
            acc = [zero, zero]
            for k in range(A_HEAD_DIM):
                acc[k % 2] = acc[k % 2] + s_scr[k] * row(t0, 2, k)

            def step(i, nacc):
                t = t0 + i
                tn = jnp.minimum(t + 1, tb - 1)
                sa = -nacc
                vv = zv_scr[pl.ds(t, _VH, stride=tb), :]
                out = [zero, zero]
                nxt = [zero, zero]
                for k in range(A_HEAD_DIM):
                    s_new = s_scr[k] + sa * row(t, 3, k) + vv * row(t, 4, k)
                    s_scr[k] = s_new
                    out[k % 2] = out[k % 2] + s_new * row(t, 1, k)
                    nxt[k % 2] = nxt[k % 2] + s_new * row(tn, 2, k)
                o_scr[pl.ds(pl.multiple_of(t * _VH, _VH), _VH), :] = out[0] + out[1]
                return nxt[0] + nxt[1]

            lax.fori_loop(0, _SCAN_WIN, step, acc[0] + acc[1])
            t_end = t0 + _SCAN_WIN - 1
            for k in range(A_HEAD_DIM):
                s_scr[k] = s_scr[k] * row(t_end, 0, k)
            return carry

        lax.fori_loop(0, n_win, window, 0)
        for v in range(_VH):
            mt = o_scr[pl.ds(v, tb, stride=_VH), :].T
            for half in range(2):
                for b in range(SCAN_GROUP):
                    src = (half * SCAN_GROUP + b) * A_HEADS
                    dst = (half * _VH + v) * A_HEADS
                    o_ref[b, dst:dst + A_HEADS, :] = mt[src:src + A_HEADS, :]
        sout_ref[...] = s_scr[...]


def _scan_cm(ops, seq_len, tb):
    nt = seq_len // tb
    n_phase = len(_PHASE_OPS) + 1

    def op_of_phase(p):
        op = _PHASE_OPS[-1]
        for i in range(len(_PHASE_OPS) - 2, -1, -1):
            op = jnp.where(p == i, _PHASE_OPS[i], op)
        return op

    def in_spec(b):
        return pl.BlockSpec((1, A_WIDTH, tb), lambda t, p: (op_of_phase(p), 0, b * nt + t))

    return pl.pallas_call(
        functools.partial(_scan_cm_kernel, tb=tb),
        grid=(nt, n_phase),
        in_specs=[in_spec(b) for b in range(SCAN_GROUP)],
        out_specs=[pl.BlockSpec((SCAN_GROUP, A_WIDTH, tb), lambda t, p: (0, 0, t)),
                   pl.BlockSpec((A_HEAD_DIM, _VH, LANES), lambda t, p: (0, 0, 0))],
        out_shape=[jax.ShapeDtypeStruct((SCAN_GROUP, A_WIDTH, seq_len), _F32),
                   jax.ShapeDtypeStruct((A_HEAD_DIM, _VH, LANES), _F32)],
        scratch_shapes=[pltpu.VMEM((5 * A_HEAD_DIM, tb, LANES), _F32),
                        pltpu.VMEM((_VH * tb, LANES), _F32),
                        pltpu.VMEM((tb * _VH, LANES), _F32),
                        pltpu.VMEM((A_HEAD_DIM, _VH, LANES), _F32),
                        pltpu.VMEM((3 * A_HEAD_DIM, SCAN_GROUP * A_HEADS, tb), _F32)],
        compiler_params=_cparams("arbitrary", "arbitrary"), name="rwkv_scan_cm",
    )(*([ops] * SCAN_GROUP))


def _gla_kernel(*refs, nb, chunk, valid, has_s0):
    it = iter(refs)
    p_ref = next(it)
    s0_ref = next(it) if has_s0 else None
    fup_ref = next(it)
    fb_ref = next(it)
    ng_ref = next(it)
    o_ref = next(it)
    sout_ref = next(it)
    s_scr = next(it)

    c_idx = pl.program_id(1)

    @pl.when(c_idx == 0)
    def _():
        if has_s0:
            s_scr[...] = s0_ref[0]
        else:
            s_scr[...] = jnp.zeros_like(s_scr)

    row = lax.broadcasted_iota(jnp.int32, (chunk, chunk), 0)
    col = lax.broadcasted_iota(jnp.int32, (chunk, chunk), 1)
    causal = row >= col
    eye_k = (lax.broadcasted_iota(jnp.int32, (B_DK, B_DK), 0)
             == lax.broadcasted_iota(jnp.int32, (B_DK, B_DK), 1))
    rows = nb * chunk
    trow = lax.broadcasted_iota(jnp.int32, (rows, B_KW), 0) % chunk
    ng = ng_ref[...]

    q_all = p_ref[:, :, 0:B_KW].reshape(rows, B_KW)
    k_all = p_ref[:, :, B_KW:2 * B_KW].reshape(rows, B_KW)
    f_lo = p_ref[:, :, 2 * B_KW + 2 * B_WIDTH:GLA_IN_W].reshape(rows, B_GATE_LORA)
    gk = jax.nn.log_sigmoid(_dot(f_lo, fup_ref[...]) + fb_ref[...]) / GLA_LOGIT_NORM
    if valid < chunk:
        gk = jnp.where(trow < valid, gk, 0.0)
        k_all = jnp.where(trow < valid, k_all, 0.0)
    bc = gk
    s = 1
    while s < chunk:
        bc = bc + jnp.where(trow >= s, pltpu.roll(bc, s, 0), 0.0)
        s *= 2
    b_last3 = bc.reshape(nb, chunk, B_KW)[:, chunk - 1:chunk, :]
    b_last = jnp.broadcast_to(b_last3, (nb, chunk, B_KW)).reshape(rows, B_KW)
    e_last3 = jnp.exp(b_last3)
    qt_all = q_all * (B_DK ** -0.5) * jnp.exp(bc)
    kt_all = k_all * jnp.exp(-bc)
    kd_all = k_all * jnp.exp(b_last - bc)

    units = [(b, h) for b in range(nb) for h in range(B_HEADS)]
    group = 16
    for u0 in range(0, len(units), group):
        grp = units[u0:u0 + group]
        att, qs, kv, vbs, sts = [], [], [], [], []
        for b, h in grp:
            rs = slice(b * chunk, (b + 1) * chunk)
            ks = slice(h * B_DK, (h + 1) * B_DK)
            vs = slice(2 * B_KW + h * B_DV, 2 * B_KW + (h + 1) * B_DV)
            qt = qt_all[rs, ks].astype(_BF16)
            kt = kt_all[rs, ks].astype(_BF16)
            kd = kd_all[rs, ks].astype(_BF16)
            vb = p_ref[b, :, vs].astype(_BF16)
            st = s_scr[b, h]
            att.append(lax.dot_general(qt, kt, (((1,), (1,)), ((), ())),
                                       preferred_element_type=_F32))
            qs.append(jnp.dot(qt, st.astype(_BF16), preferred_element_type=_F32))
            kv.append(lax.dot_general(kd, vb, (((0,), (0,)), ((), ())),
                                      preferred_element_type=_F32))
            vbs.append(vb)
            sts.append(st)
        outs = []
        for i in range(len(grp)):
            a = jnp.where(causal, att[i], 0.0).astype(_BF16)
            outs.append(jnp.dot(a, vbs[i], preferred_element_type=_F32) + qs[i])
        for i, (b, h) in enumerate(grp):
            ks = slice(h * B_DK, (h + 1) * B_DK)
            gs = slice(2 * B_KW + B_WIDTH + h * B_DV, 2 * B_KW + B_WIDTH + (h + 1) * B_DV)
            e_col = jnp.sum(jnp.where(eye_k, jnp.broadcast_to(e_last3[b, :, ks], (B_DK, B_DK)), 0.0),
                            axis=1, keepdims=True)
            s_scr[b, h] = sts[i] * e_col + kv[i]
            o = outs[i]
            gh = p_ref[b, :, gs]
            on = o * lax.rsqrt(jnp.mean(o * o, -1, keepdims=True) + RMS_EPS) * ng
            o_ref[b, :, h * B_DV:(h + 1) * B_DV] = on * (gh * jax.nn.sigmoid(gh))

    @pl.when(c_idx == pl.num_programs(1) - 1)
    def _():
        sout_ref[...] = s_scr[...]


def _gla(pb3, s0_all, layer, fup, fb, ng, *, nb, chunk, valid):
    n_seq, seq_len = pb3.shape[0], pb3.shape[1]
    has_s0 = s0_all is not None
    in_specs = [pl.BlockSpec((nb, chunk, GLA_IN_PAD), lambda i, c: (i, c, 0))]
    args = [pb3]
    if has_s0:
        in_specs.append(pl.BlockSpec((1, nb, B_HEADS, B_DK, B_DV),
                                     lambda i, c: (layer, i, 0, 0, 0)))
        args.append(s0_all)
    in_specs += [pl.BlockSpec((B_GATE_LORA, B_KW), lambda i, c: (0, 0)),
                 pl.BlockSpec((1, B_KW), lambda i, c: (0, 0)),
                 pl.BlockSpec((1, B_DV), lambda i, c: (0, 0))]
    args += [fup, fb, ng]
    return pl.pallas_call(
        functools.partial(_gla_kernel, nb=nb, chunk=chunk, valid=valid, has_s0=has_s0),
        grid=(n_seq // nb, seq_len // chunk),
        in_specs=in_specs,
        out_specs=[pl.BlockSpec((nb, chunk, B_WIDTH), lambda i, c: (i, c, 0)),
                   pl.BlockSpec((nb, B_HEADS, B_DK, B_DV), lambda i, c: (i, 0, 0, 0))],
        out_shape=[jax.ShapeDtypeStruct((n_seq, seq_len, B_WIDTH), _F32),
                   jax.ShapeDtypeStruct((n_seq, B_HEADS, B_DK, B_DV), _F32)],
        scratch_shapes=[pltpu.VMEM((nb, B_HEADS, B_DK, B_DV), _F32)],
        compiler_params=_cparams("parallel", "arbitrary"), name="gla",
    )(*args)


def _outproj_kernel(x_ref, o_ref, gate_ref, bonus_ref, lng_ref, lnb_ref, ob_ref, wa_ref, wb_ref,
                    g_ref, b_ref, y_ref, *, tm):
    o3 = o_ref[...].reshape(A_HEAD_DIM, A_HEADS, tm)
    m = jnp.mean(o3, axis=0, keepdims=True)
    oc = o3 - m
    var = jnp.mean(oc * oc, axis=0, keepdims=True)
    on = (oc * lax.rsqrt(var + GN_EPS_A)).reshape(A_WIDTH, tm)
    oa = ((on * lng_ref[...] + lnb_ref[...] + bonus_ref[...]) * gate_ref[...]).T
    mix = _dot(oa, wa_ref[0]) + _dot(ob_ref[...], wb_ref[0])
    y_ref[...] = _layer_norm(ALPHA * x_ref[...] + mix, g_ref[...], b_ref[...])


def _outproj(x, o_cm, gate, bonus, lng, lnb, ob, wa_all, wb_all, layer, g, b, tm):
    n = x.shape[0]
    rowb = lambda w: pl.BlockSpec((tm, w), lambda i: (i, 0))
    const = lambda r, c: pl.BlockSpec((r, c), lambda i: (0, 0))
    wspec = pl.BlockSpec((1, A_WIDTH, D_MODEL), lambda i: (layer, 0, 0))
    cm = pl.BlockSpec((A_WIDTH, tm), lambda i: (0, i))
    if o_cm.ndim == 3:
        nt = o_cm.shape[2] // tm
        o_spec = pl.BlockSpec((1, A_WIDTH, tm), lambda i: (i // nt, 0, i % nt))
    else:
        o_spec = cm
    return pl.pallas_call(
        functools.partial(_outproj_kernel, tm=tm),
        grid=(n // tm,),
        in_specs=[rowb(D_MODEL), o_spec, cm, cm, const(A_WIDTH, tm), const(A_WIDTH, tm),
                  rowb(B_WIDTH), wspec, wspec, const(1, D_MODEL), const(1, D_MODEL)],
        out_specs=rowb(D_MODEL),
        out_shape=jax.ShapeDtypeStruct((n, D_MODEL), _F32),
        compiler_params=_cparams("parallel"), name="outproj_ln",
    )(x, o_cm, gate, bonus, lng, lnb, ob, wa_all, wb_all, g, b)


def _ffn_kernel(*refs, tm, period):
    it = iter(refs)
    x_ref = next(it)
    p1_ref = next(it) if period is not None else None
    p2_ref = next(it) if period is not None else None
    wup_ref = next(it)
    cw_ref = next(it)
    cb_ref = next(it)
    wdn_ref = next(it)
    g_ref = next(it)
    b_ref = next(it)
    y_ref = next(it)
    gate_ref = next(it)
    carry_ref = next(it) if period is None else None

    x = x_ref[...]
    u = jnp.dot(x.astype(_BF16), wup_ref[0], preferred_element_type=_F32)
    gate = u[:, :D_FF]
    val = u[:, D_FF:]
    row = lax.broadcasted_iota(jnp.int32, (tm, D_FF), 0)
    r1 = pltpu.roll(gate, 1, 0)
    r2 = pltpu.roll(gate, 2, 0)
    if period is None:
        @pl.when(pl.program_id(1) == 0)
        def _():
            carry_ref[...] = jnp.zeros_like(carry_ref)
        c = carry_ref[...]
        row8 = lax.broadcasted_iota(jnp.int32, (SUBLANES, D_FF), 0)
        top1 = jnp.where(row8 < 1, pltpu.roll(c, 1, 0), r1[:SUBLANES])
        top2 = jnp.where(row8 < 2, pltpu.roll(c, 2, 0), r2[:SUBLANES])
        g1 = jnp.concatenate([top1, r1[SUBLANES:]], axis=0)
        g2 = jnp.concatenate([top2, r2[SUBLANES:]], axis=0)
        carry_ref[...] = gate[tm - SUBLANES:]
        gate_ref[0] = gate[tm - SUBLANES:]
    else:
        g1 = jnp.where(row % period >= 1, r1, p1_ref[...])
        g2 = jnp.where(row % period >= 2, r2, p2_ref[...])
        gate_ref[...] = gate
    acc = cb_ref[...] + g2 * cw_ref[0:1, :] + g1 * cw_ref[1:2, :] + gate * cw_ref[2:3, :]
    h = 0.5 * acc * (1.0 + lax.erf(acc * (2.0 ** -0.5))) * val
    f = jnp.dot(h.astype(_BF16), wdn_ref[0], preferred_element_type=_F32)
    y_ref[...] = _layer_norm(ALPHA * x + f, g_ref[...], b_ref[...])


def _ffn(x, p1, p2, wup_all, cw, cb, wdn_all, layer, g, b, *, n_seq, seq_len, tm):
    n = x.shape[0]
    prompt = p1 is None
    if prompt:
        nt = seq_len // tm
        grid = (n_seq, nt)
        rmap = lambda bb, t: (bb * nt + t, 0)
        sem = ("parallel", "arbitrary")
    else:
        grid = (n // tm,)
        rmap = lambda i: (i, 0)
        sem = ("parallel",)
    const = lambda r, c: pl.BlockSpec((r, c), lambda *gidx: (0, 0), pipeline_mode=pl.Buffered(1))
    in_specs = [pl.BlockSpec((tm, D_MODEL), rmap)]
    args = [x]
    if not prompt:
        in_specs += [pl.BlockSpec((tm, D_FF), rmap), pl.BlockSpec((tm, D_FF), rmap)]
        args += [p1, p2]
    wspec = lambda r, c: pl.BlockSpec((1, r, c), lambda *gidx: (layer, 0, 0),
                                      pipeline_mode=pl.Buffered(1))
    in_specs += [wspec(D_MODEL, 2 * D_FF), const(CONV_W, D_FF), const(1, D_FF),
                 wspec(D_FF, D_MODEL), const(1, D_MODEL), const(1, D_MODEL)]
    args += [wup_all, cw, cb, wdn_all, g, b]
    if prompt:
        gate_spec = pl.BlockSpec((1, SUBLANES, D_FF), lambda bb, t: (bb, 0, 0))
        gate_shape = jax.ShapeDtypeStruct((n_seq, SUBLANES, D_FF), _F32)
        scratch = [pltpu.VMEM((SUBLANES, D_FF), _F32)]
    else:
        gate_spec = pl.BlockSpec((tm, D_FF), rmap)
        gate_shape = jax.ShapeDtypeStruct((n, D_FF), _F32)
        scratch = []
    return pl.pallas_call(
        functools.partial(_ffn_kernel, tm=tm, period=None if prompt else seq_len),
        grid=grid, in_specs=in_specs,
        out_specs=[pl.BlockSpec((tm, D_MODEL), rmap), gate_spec],
        out_shape=[jax.ShapeDtypeStruct((n, D_MODEL), _F32), gate_shape],
        scratch_shapes=scratch, compiler_params=_cparams(*sem), name="ffn_ln",
    )(*args)


def _ops_to_short(ops, n_grp, seq_len):
    n_b = SCAN_ROWS // A_HEADS
    x = ops.reshape(6, A_HEAD_DIM, A_HEADS, n_grp, n_b, seq_len)
    z = jnp.transpose(x[:_OP_V], (3, 5, 0, 1, 4, 2)).reshape(n_grp, seq_len, 5 * A_HEAD_DIM, LANES)
    v = jnp.transpose(x[_OP_V], (2, 4, 0, 3, 1)).reshape(n_grp, seq_len, A_HEAD_DIM, LANES)
    return z, v


def _short_to_cm(o, n_grp, seq_len):
    n_b = SCAN_ROWS // A_HEADS
    x = o.reshape(n_grp, seq_len, A_HEAD_DIM, n_b, A_HEADS)
    return jnp.transpose(x, (2, 4, 0, 3, 1)).reshape(A_WIDTH, n_grp * n_b * seq_len)


def _lanes_to_state(s):
    x = s.reshape(A_HEAD_DIM, _VH, 2, SCAN_GROUP, A_HEADS)
    return jnp.transpose(x, (3, 4, 2, 1, 0)).reshape(SCAN_GROUP, A_HEADS, A_HEAD_DIM, A_HEAD_DIM)


def _kh_rows(x):
    return jnp.swapaxes(x.reshape((A_HEADS, A_HEAD_DIM) + x.shape[1:]), 0, 1).reshape(x.shape)


def _rwkv_rows(x):
    blocks = [_kh_rows(x[i * A_WIDTH:(i + 1) * A_WIDTH]) for i in range(3)]
    return jnp.concatenate(blocks + [x[3 * A_WIDTH:]], axis=0)


def _rwkv_rows_inv(x):
    def hk(y):
        return jnp.swapaxes(y.reshape((A_HEAD_DIM, A_HEADS) + y.shape[1:]), 0, 1).reshape(y.shape)
    blocks = [hk(x[i * A_WIDTH:(i + 1) * A_WIDTH]) for i in range(3)]
    return jnp.concatenate(blocks + [x[3 * A_WIDTH:]], axis=0)


def _bcast_cols(p, tb):
    return jnp.broadcast_to(p[:, None], (p.shape[0], tb))


def _stacked_weights(P):
    w_in = P['w_in']
    wta = jax.vmap(_rwkv_rows)(jnp.swapaxes(w_in[:, :, :SHIFT_W], 1, 2)).astype(_BF16)
    wb = jnp.pad(w_in[:, :, SHIFT_W:], ((0, 0), (0, 0), (0, GLA_IN_PAD - GLA_IN_W))).astype(_BF16)
    woa = jax.vmap(_kh_rows)(P['w_out'][:, :A_WIDTH]).astype(_BF16)
    wob = P['w_out'][:, A_WIDTH:].astype(_BF16)
    return dict(wta=wta, wb=wb, woa=woa, wob=wob,
                w_up=P['w_up'].astype(_BF16), w_down=P['w_down'].astype(_BF16))


def _layer_weights(P, l, tb_prep, tb_post):
    vb = _kh_rows(P['vres_bias'][l - 1]) if l > 0 else jnp.zeros((A_WIDTH,), _F32)
    prm = jnp.stack([_kh_rows(P['w0'][l]), _kh_rows(P['a0'][l]), _kh_rows(P['k_k'][l]),
                     _kh_rows(P['k_a'][l]), _kh_rows(P['r_k'][l].reshape(A_WIDTH)), vb])
    prm = jnp.broadcast_to(prm[:, :, None], (6, A_WIDTH, tb_prep))
    if l > 0:
        v1 = _kh_rows(P['vres_down'][l - 1]).T.astype(_BF16)
        v2 = _kh_rows(P['vres_up'][l - 1].T).astype(_BF16)
    else:
        v1 = jnp.zeros((A_MV_LORA, A_WIDTH), _BF16)
        v2 = jnp.zeros((A_WIDTH, A_MV_LORA), _BF16)
    return dict(
        mu=_bcast_cols(_rwkv_rows(P['tok_mu'][l]), tb_prep), prm=prm,
        wup=_kh_rows(P['w_lora_up'][l].T).astype(_BF16),
        aup=_kh_rows(P['a_lora_up'][l].T).astype(_BF16),
        gup=_kh_rows(P['g_lora_up'][l].T).astype(_BF16),
        v1=v1, v2=v2,
        lng=_bcast_cols(_kh_rows(P['lnx_g'][l]), tb_post),
        lnb=_bcast_cols(_kh_rows(P['lnx_b'][l]), tb_post),
        fup=P['gla_f_up'][l].astype(_BF16), fb=P['gla_f_bias'][l][None], ng=P['gla_norm_g'][l][None],
        ln1g=P['ln1_g'][l][None], ln1b=P['ln1_b'][l][None],
        cw=P['conv_w'][l], cb=P['conv_b'][l][None],
        ln2g=P['ln2_g'][l][None], ln2b=P['ln2_b'][l][None])


def _trunk(x3, states, P, SW, prompt):
    n_seq, seq_len, _ = x3.shape
    n = n_seq * seq_len
    x = x3.reshape(n, D_MODEL)
    tb = 2 * LANES
    if prompt:
        scan_tb, gla_chunk, gla_valid, gla_t, tm_out, tm_ffn = LANES, GLA_CHUNK, GLA_CHUNK, seq_len, 256, 256
        st_gla = None
    else:
        st_rwkv, st_shift, st_gla, st_conv = states
        scan_tb, gla_chunk, gla_valid, gla_t, tm_out, tm_ffn = seq_len, SUBLANES, seq_len, SUBLANES, 256, 256
        n_grp = n_seq * A_HEADS // SCAN_ROWS
    new_rwkv, new_shift, new_gla, new_conv = [], [], [], []
    ops0 = None
    for l in range(DEPTH):
        W = _layer_weights(P, l, tb, tm_out)
        if prompt:
            bnd = None
        else:
            bnd = jnp.repeat(_rwkv_rows(st_shift[l][:, 0, :].T), seq_len, axis=1)
        ops, g, bonus, pb, pa_out = _prep(
            x, SW['wta'], SW['wb'], l, bnd, ops0, W['mu'], W['prm'], W['wup'], W['aup'], W['gup'],
            W['v1'], W['v2'], n_seq=n_seq, seq_len=seq_len, tb=tb)
        if l == 0:
            ops0 = ops
        if prompt:
            o_cm, s_fin = _scan_cm(ops, seq_len, scan_tb)
            s_fin = _lanes_to_state(s_fin)
            last_cols = pa_out[:, :, tb - 1].T
        else:
            z, zv = _ops_to_short(ops, n_grp, seq_len)
            o, s_fin = _scan_short(st_rwkv, l, z, zv)
            o_cm = _short_to_cm(o, n_grp, seq_len)
            last_cols = pa_out[:, seq_len - 1::seq_len]
        pb3 = pb.reshape(n_seq, seq_len, GLA_IN_PAD)
        if gla_t != seq_len:
            pb3 = jnp.pad(pb3, ((0, 0), (0, gla_t - seq_len), (0, 0)))
        ob, s_gla = _gla(pb3, st_gla, l, W['fup'], W['fb'], W['ng'],
                         nb=SCAN_GROUP, chunk=gla_chunk, valid=gla_valid)
        ob = ob[:, :seq_len].reshape(n, B_WIDTH)
        x1 = _outproj(x, o_cm, g, bonus, W['lng'], W['lnb'], ob, SW['woa'], SW['wob'], l,
                      W['ln1g'], W['ln1b'], tm_out)
        if prompt:
            p1 = p2 = None
        else:
            past = st_conv[l]
            zero = jnp.zeros((n_seq, 1, D_FF), _F32)
            p1 = jnp.concatenate([past[:, 1:2], zero, zero, zero], axis=1).reshape(n, D_FF)
            p2 = jnp.concatenate([past[:, 0:1], past[:, 1:2], zero, zero], axis=1).reshape(n, D_FF)
        x, gate = _ffn(x1, p1, p2, SW['w_up'], W['cw'], W['cb'], SW['w_down'], l,
                       W['ln2g'], W['ln2b'], n_seq=n_seq, seq_len=seq_len, tm=tm_ffn)
        new_rwkv.append(s_fin)
        new_shift.append(_rwkv_rows_inv(last_cols).T[:, None, :])
        new_gla.append(s_gla)
        if prompt:
            new_conv.append(gate[:, SUBLANES - (CONV_W - 1):])
        else:
            new_conv.append(gate.reshape(n_seq, seq_len, D_FF)[:, seq_len - (CONV_W - 1):])
    return (x.reshape(n_seq, seq_len, D_MODEL), jnp.stack(new_rwkv), jnp.stack(new_shift),
            jnp.stack(new_gla), jnp.stack(new_conv))


def kernel(x_prompt, x_sample, state_rwkv, state_shift, state_gla, state_conv, w_in, tok_mu, w0, w_lora_up, a0, a_lora_up, g_lora_up, k_k, k_a, r_k, lnx_g, lnx_b, vres_bias, vres_down, vres_up, gla_f_up, gla_f_bias, gla_norm_g, w_out, ln1_g, ln1_b, w_up, conv_w, conv_b, w_down, ln2_g, ln2_b):
    P = dict(w_in=w_in, tok_mu=tok_mu, w0=w0, w_lora_up=w_lora_up, a0=a0, a_lora_up=a_lora_up,
             g_lora_up=g_lora_up, k_k=k_k, k_a=k_a, r_k=r_k, lnx_g=lnx_g, lnx_b=lnx_b,
             vres_bias=vres_bias, vres_down=vres_down, vres_up=vres_up, gla_f_up=gla_f_up,
             gla_f_bias=gla_f_bias, gla_norm_g=gla_norm_g, w_out=w_out, ln1_g=ln1_g, ln1_b=ln1_b,
             w_up=w_up, conv_w=conv_w, conv_b=conv_b, w_down=w_down, ln2_g=ln2_g, ln2_b=ln2_b)
    SW = _stacked_weights(P)
    y_p, rwkv_p, shift_p, gla_p, conv_p = _trunk(x_prompt, None, P, SW, True)
    y_s, rwkv_s, shift_s, gla_s, conv_s = _trunk(
        x_sample, (state_rwkv, state_shift, state_gla, state_conv), P, SW, False)
    return (y_p, y_s, rwkv_p, rwkv_s, shift_p, shift_s, gla_p, gla_s, conv_p, conv_s)
```

```python
import functools

import jax
import jax.numpy as jnp
from jax import lax
from jax.experimental import pallas as pl
from jax.experimental.pallas import tpu as pltpu

D_MODEL = 1024
DEPTH = 4
A_WIDTH = 512
A_HEAD_DIM = 64
A_HEADS = 8
A_DECAY_LORA = 64
A_AAA_LORA = 64
A_MV_LORA = 32
A_GATE_LORA = 128
GN_EPS_A = 64e-5
B_WIDTH = 512
B_HEADS = 4
B_DV = 128
B_DK = 64
B_KW = 256
B_GATE_LORA = 16
GLA_LOGIT_NORM = 16.0
GLA_CHUNK = 64
RMS_EPS = 1e-5
D_FF = 2816
CONV_W = 3
ALPHA = (2 * DEPTH) ** 0.25
LN_EPS = 1e-5
SHIFT_W = 3 * A_WIDTH + A_DECAY_LORA + A_AAA_LORA + A_GATE_LORA
GLA_IN_W = 2 * B_KW + 2 * B_WIDTH + B_GATE_LORA
GLA_IN_PAD = 1664

LANES = 128
SUBLANES = 8
SCAN_GROUP = 8
VMEM_LIMIT = 56 * 1024 * 1024


_F32 = jnp.float32
_BF16 = jnp.bfloat16


def _cparams(*sem):
    return pltpu.CompilerParams(dimension_semantics=sem, vmem_limit_bytes=VMEM_LIMIT)


def _dot(a, b):
    return jnp.dot(a.astype(_BF16), b.astype(_BF16), preferred_element_type=_F32)


def _layer_norm(y, g, b):
    mu = jnp.mean(y, -1, keepdims=True)
    yc = y - mu
    var = jnp.mean(yc * yc, -1, keepdims=True)
    return yc * lax.rsqrt(var + LN_EPS) * g + b


def _head_sum(x, tb):
    return jnp.sum(x.reshape(A_HEAD_DIM, A_HEADS, tb), axis=0)


def _head_bcast(s, tb):
    return jnp.broadcast_to(s[None], (A_HEAD_DIM, A_HEADS, tb)).reshape(A_WIDTH, tb)


def _prep_kernel(*refs, tb, period, has_vmix):
    it = iter(refs)
    x_ref = next(it)
    wta_ref = next(it)
    wb_ref = next(it)
    bnd_ref = next(it) if period is not None else None
    vfirst_ref = next(it) if has_vmix else None
    mu_ref = next(it)
    prm_ref = next(it)
    wup_ref = next(it)
    aup_ref = next(it)
    gup_ref = next(it)
    v1_ref = next(it)
    v2_ref = next(it)
    ops_ref = next(it)
    g_ref = next(it)
    bonus_ref = next(it)
    pb_ref = next(it)
    pa_out_ref = next(it)
    carry_ref = next(it)

    x = x_ref[...].astype(_BF16)
    pb_ref[...] = jnp.dot(x, wb_ref[0], preferred_element_type=_F32)
    pa = lax.dot_general(wta_ref[0], x, (((1,), (1,)), ((), ())), preferred_element_type=_F32)
    if period is None:
        lane = lax.broadcasted_iota(jnp.int32, (SHIFT_W, tb), 1)

        @pl.when(pl.program_id(1) == 0)
        def _():
            carry_ref[...] = jnp.zeros_like(carry_ref)
        prev = jnp.where(lane == 0, pltpu.roll(carry_ref[...], 1, 1), pltpu.roll(pa, 1, 1))
        carry_ref[...] = pa
        pa_out_ref[0] = pa
    else:
        @pl.when(pl.program_id(0) == 0)
        def _():
            carry_ref[...] = bnd_ref[...]
        prev = jnp.concatenate([carry_ref[...], pa[:, :tb - period]], axis=1)
        carry_ref[...] = pa[:, tb - period:]
        pa_out_ref[...] = pa

    xs = pa + (prev - pa) * mu_ref[...]
    r = xs[0:A_WIDTH]
    k = xs[A_WIDTH:2 * A_WIDTH]
    v = xs[2 * A_WIDTH:3 * A_WIDTH]
    o0 = 3 * A_WIDTH
    w_lo = xs[o0:o0 + A_DECAY_LORA]
    a_lo = xs[o0 + A_DECAY_LORA:o0 + A_DECAY_LORA + A_AAA_LORA]
    g_lo = xs[o0 + A_DECAY_LORA + A_AAA_LORA:SHIFT_W]

    w0, a0, k_k, k_a, r_k = prm_ref[0], prm_ref[1], prm_ref[2], prm_ref[3], prm_ref[4]
    w = -jax.nn.softplus(-(w0 + _dot(wup_ref[...], jnp.tanh(w_lo)))) - 0.5
    log_decay = -jnp.exp(w)
    decay = log_decay if period is None else jnp.exp(log_decay)
    a = jax.nn.sigmoid(a0 + _dot(aup_ref[...], a_lo))
    g = _dot(gup_ref[...], jax.nn.sigmoid(g_lo))
    if has_vmix:
        mix = jax.nn.sigmoid(prm_ref[5] + _dot(v2_ref[...], _dot(v1_ref[...], v)))
        v = v + (vfirst_ref[0] - v) * mix
    kk = k * k_k
    ss = _head_sum(kk * kk, tb)
    kk = kk * _head_bcast(lax.rsqrt(jnp.maximum(ss, 1e-24)), tb)
    kx = k * (1.0 + (a - 1.0) * k_a)
    ops_ref[0] = r
    ops_ref[1] = kk
    ops_ref[2] = decay
    ops_ref[3] = kk * a
    ops_ref[4] = kx
    ops_ref[5] = v
    g_ref[...] = g
    bonus_ref[...] = _head_bcast(_head_sum(r * kx * r_k, tb), tb) * v


def _prep(x, wta_all, wb_all, layer, bnd, vfirst_ops, mu, prm, wup, aup, gup, v1, v2, *,
          n_seq, seq_len, tb):
    n = x.shape[0]
    prompt = bnd is None
    has_vmix = vfirst_ops is not None
    if prompt:
        nt = seq_len // tb
        grid = (n_seq, nt)
        tok = lambda b, t: b * nt + t
        sem = ("parallel", "arbitrary")
    else:
        assert tb % n_seq == 0 and tb > n_seq
        grid = (n // tb,)
        tok = lambda i: i
        sem = ("arbitrary",)
    cmap = lambda f: (lambda *g: f(tok(*g)))
    in_specs = [pl.BlockSpec((tb, D_MODEL), cmap(lambda j: (j, 0))),
                pl.BlockSpec((1, SHIFT_W, D_MODEL), lambda *g: (layer, 0, 0)),
                pl.BlockSpec((1, D_MODEL, GLA_IN_PAD), lambda *g: (layer, 0, 0))]
    args = [x, wta_all, wb_all]
    if not prompt:
        in_specs.append(pl.BlockSpec((SHIFT_W, n_seq), lambda i: (0, 0)))
        args.append(bnd)
    if has_vmix:
        in_specs.append(pl.BlockSpec((1, A_WIDTH, tb), cmap(lambda j: (5, 0, j))))
        args.append(vfirst_ops)
    const2 = lambda *g: (0, 0)
    const3 = lambda *g: (0, 0, 0)
    in_specs += [pl.BlockSpec((SHIFT_W, tb), const2),
                 pl.BlockSpec((6, A_WIDTH, tb), const3),
                 pl.BlockSpec((A_WIDTH, A_DECAY_LORA), const2),
                 pl.BlockSpec((A_WIDTH, A_AAA_LORA), const2),
                 pl.BlockSpec((A_WIDTH, A_GATE_LORA), const2),
                 pl.BlockSpec((A_MV_LORA, A_WIDTH), const2),
                 pl.BlockSpec((A_WIDTH, A_MV_LORA), const2)]
    args += [mu, prm, wup, aup, gup, v1, v2]
    out_specs = [pl.BlockSpec((6, A_WIDTH, tb), cmap(lambda j: (0, 0, j))),
                 pl.BlockSpec((A_WIDTH, tb), cmap(lambda j: (0, j))),
                 pl.BlockSpec((A_WIDTH, tb), cmap(lambda j: (0, j))),
                 pl.BlockSpec((tb, GLA_IN_PAD), cmap(lambda j: (j, 0)))]
    out_shape = [jax.ShapeDtypeStruct((6, A_WIDTH, n), _F32),
                 jax.ShapeDtypeStruct((A_WIDTH, n), _F32),
                 jax.ShapeDtypeStruct((A_WIDTH, n), _F32),
                 jax.ShapeDtypeStruct((n, GLA_IN_PAD), _F32)]
    scratch = [pltpu.VMEM((SHIFT_W, tb if prompt else n_seq), _F32)]
    if prompt:
        out_specs.append(pl.BlockSpec((1, SHIFT_W, tb), lambda b, t: (b, 0, 0)))
        out_shape.append(jax.ShapeDtypeStruct((n_seq, SHIFT_W, tb), _F32))
    else:
        out_specs.append(pl.BlockSpec((SHIFT_W, tb), cmap(lambda j: (0, j))))
        out_shape.append(jax.ShapeDtypeStruct((SHIFT_W, n), _F32))
    return pl.pallas_call(
        functools.partial(_prep_kernel, tb=tb, period=None if prompt else n_seq,
                          has_vmix=has_vmix),
        grid=grid, in_specs=in_specs, out_specs=out_specs, out_shape=out_shape,
        scratch_shapes=scratch, compiler_params=_cparams(*sem), name="rwkv_prep",
    )(*args)


_VH = A_HEAD_DIM // 2


_OP_R, _OP_KK, _OP_DECAY, _OP_B, _OP_K, _OP_V = range(6)


def _scan_short_kernel(s_ref, *refs, n_steps, n_seq):
    step_refs = refs[:n_steps]
    o_ref, sout_ref, s_scr = refs[n_steps:]
    head = pl.program_id(0)
    for v in range(0, A_HEAD_DIM, 2):
        pair = [s_ref[0, :, 0, v + j, :] for j in range(2)]
        s_scr[pl.ds(v * A_HEAD_DIM, 2 * A_HEAD_DIM), :] = jnp.concatenate(pair, axis=1).T

    for t in range(n_steps):
        lanes = slice(t * n_seq, (t + 1) * n_seq)
        ops_ref = step_refs[t]

        def operand(op):
            return ops_ref[op, pl.ds(head, A_HEAD_DIM, stride=A_HEADS), :]

        r, kk, decay, b, kx = (operand(op) for op in (_OP_R, _OP_KK, _OP_DECAY, _OP_B, _OP_K))
        for v in range(A_HEAD_DIM):
            rows = pl.ds(v * A_HEAD_DIM, A_HEAD_DIM)
            s = s_scr[rows, :]
            sa = -jnp.sum(s * kk, axis=0, keepdims=True)
            vrow = ops_ref[_OP_V, pl.ds(v * A_HEADS + head, 1), :]
            s_new = s * decay + sa * b + vrow * kx
            s_scr[rows, :] = s_new
            o_ref[0, pl.ds(v, 1), lanes] = jnp.sum(s_new * r, axis=0, keepdims=True)

    for v in range(0, A_HEAD_DIM, 2):
        pair = s_scr[pl.ds(v * A_HEAD_DIM, 2 * A_HEAD_DIM), :].T
        for j in range(2):
            sout_ref[:, 0, v + j, :] = pair[:, j * A_HEAD_DIM:(j + 1) * A_HEAD_DIM]


def _scan_short(s_all, layer, ops, n_seq):
    n = ops.shape[2]
    n_steps = n // n_seq
    s_block = (n_seq, 1, A_HEAD_DIM, A_HEAD_DIM)

    def step_spec(t):
        return pl.BlockSpec((6, A_WIDTH, n_seq), lambda h: (0, 0, t))

    return pl.pallas_call(
        functools.partial(_scan_short_kernel, n_steps=n_steps, n_seq=n_seq),
        grid=(A_HEADS,),
        in_specs=[pl.BlockSpec((1,) + s_block, lambda h: (layer, 0, h, 0, 0))]
        + [step_spec(t) for t in range(n_steps)],
        out_specs=[pl.BlockSpec((1, A_HEAD_DIM, n), lambda h: (h, 0, 0)),
                   pl.BlockSpec(s_block, lambda h: (0, h, 0, 0))],
        out_shape=[jax.ShapeDtypeStruct((A_HEADS, A_HEAD_DIM, n), _F32),
                   jax.ShapeDtypeStruct((n_seq, A_HEADS, A_HEAD_DIM, A_HEAD_DIM), _F32)],
        scratch_shapes=[pltpu.VMEM((A_HEAD_DIM * A_HEAD_DIM, n_seq), _F32)],
        compiler_params=_cparams("parallel"), name="rwkv_scan_short",
    )(s_all, *([ops] * n_steps))


_SCAN_WIN = 32
_PHASE_OPS = (_OP_DECAY, _OP_R, _OP_KK, _OP_B, _OP_K, _OP_V)


def _scan_cm_kernel(*refs, tb):
    in_refs = refs[:SCAN_GROUP]
    o_ref, sout_ref, z_scr, zv_scr, o_scr, s_scr, d_scr = refs[SCAN_GROUP:]
    t_blk = pl.program_id(0)
    phase = pl.program_id(1)
    n_win = tb // _SCAN_WIN

    @pl.when(jnp.logical_and(t_blk == 0, phase == 0))
    def _():
        s_scr[...] = jnp.zeros_like(s_scr)

    def gather(k):
        rows = slice(k * A_HEADS, (k + 1) * A_HEADS)
        return jnp.concatenate([r[0, rows, :] for r in in_refs], axis=0)

    @pl.when(phase == 0)
    def _():
        kg = 8
        nrow = SCAN_GROUP * A_HEADS
        lane = lax.broadcasted_iota(jnp.int32, (kg * nrow, tb), 1) % _SCAN_WIN
        for k0 in range(0, A_HEAD_DIM, kg):
            logd = jnp.concatenate([gather(k0 + j) for j in range(kg)], axis=0)
            c = logd
            s = 1
            while s < _SCAN_WIN:
                c = c + jnp.where(lane >= s, pltpu.roll(c, s, 1), 0.0)
                s *= 2
            dm = jnp.exp(c)
            di = jnp.exp(-c)
            dp = jnp.exp(c - logd)
            for j in range(kg):
                rows = slice(j * nrow, (j + 1) * nrow)
                d_scr[k0 + j] = dm[rows]
                d_scr[A_HEAD_DIM + k0 + j] = di[rows]
                d_scr[2 * A_HEAD_DIM + k0 + j] = dp[rows]
                z_scr[k0 + j] = jnp.concatenate([dm[rows], dm[rows]], axis=0).T

    @pl.when(jnp.logical_and(phase >= 1, phase <= 4))
    def _():
        fac = jnp.where(phase == 1, 0, jnp.where(phase == 2, 2, 1)) * A_HEAD_DIM
        for k in range(A_HEAD_DIM):
            m = gather(k) * d_scr[fac + k]
            z_scr[phase * A_HEAD_DIM + k] = jnp.concatenate([m, m], axis=0).T

    @pl.when(phase == 5)
    def _():
        for v in range(_VH):
            lo = slice(v * A_HEADS, (v + 1) * A_HEADS)
            hi = slice((_VH + v) * A_HEADS, (_VH + v + 1) * A_HEADS)
            m = jnp.concatenate([r[0, lo, :] for r in in_refs]
                                + [r[0, hi, :] for r in in_refs], axis=0)
            zv_scr[pl.ds(v * tb, tb), :] = m.T

    @pl.when(phase == 6)
    def _():
        def row(t, slot, k):
            return z_scr[slot * A_HEAD_DIM + k, pl.ds(t, 1), :]

        zero = jnp.zeros((_VH, LANES), _F32)

        def window(w, carry):
            t0 = w * _SCAN_WIN
            acc = [zero, zero]
            for k in range(A_HEAD_DIM):
                acc[k % 2] = acc[k % 2] + s_scr[k] * row(t0, 2, k)

            def step(i, nacc):
                t = t0 + i
                tn = jnp.minimum(t + 1, tb - 1)
                sa = -nacc
                vv = zv_scr[pl.ds(t, _VH, stride=tb), :]
                out = [zero, zero]
                nxt = [zero, zero]
                for k in range(A_HEAD_DIM):
                    s_new = s_scr[k] + sa * row(t, 3, k) + vv * row(t, 4, k)
                    s_scr[k] = s_new
                    out[k % 2] = out[k % 2] + s_new * row(t, 1, k)
                    nxt[k % 2] = nxt[k % 2] + s_new * row(tn, 2, k)
                o_scr[pl.ds(pl.multiple_of(t * _VH, _VH), _VH), :] = out[0] + out[1]
                return nxt[0] + nxt[1]

            lax.fori_loop(0, _SCAN_WIN, step, acc[0] + acc[1])
            t_end = t0 + _SCAN_WIN - 1
            for k in range(A_HEAD_DIM):
                s_scr[k] = s_scr[k] * row(t_end, 0, k)
            return carry

        lax.fori_loop(0, n_win, window, 0)
        for v in range(_VH):
            mt = o_scr[pl.ds(v, tb, stride=_VH), :].T
            for half in range(2):
                for b in range(SCAN_GROUP):
                    src = (half * SCAN_GROUP + b) * A_HEADS
                    dst = (half * _VH + v) * A_HEADS
                    o_ref[b, dst:dst + A_HEADS, :] = mt[src:src + A_HEADS, :]
        sout_ref[...] = s_scr[...]


def _scan_cm(ops, seq_len, tb):
    nt = seq_len // tb
    n_phase = len(_PHASE_OPS) + 1

    def op_of_phase(p):
        op = _PHASE_OPS[-1]
        for i in range(len(_PHASE_OPS) - 2, -1, -1):
            op = jnp.where(p == i, _PHASE_OPS[i], op)
        return op

    def in_spec(b):
        return pl.BlockSpec((1, A_WIDTH, tb), lambda t, p: (op_of_phase(p), 0, b * nt + t))

    return pl.pallas_call(
        functools.partial(_scan_cm_kernel, tb=tb),
        grid=(nt, n_phase),
        in_specs=[in_spec(b) for b in range(SCAN_GROUP)],
        out_specs=[pl.BlockSpec((SCAN_GROUP, A_WIDTH, tb), lambda t, p: (0, 0, t)),
                   pl.BlockSpec((A_HEAD_DIM, _VH, LANES), lambda t, p: (0, 0, 0))],
        out_shape=[jax.ShapeDtypeStruct((SCAN_GROUP, A_WIDTH, seq_len), _F32),
                   jax.ShapeDtypeStruct((A_HEAD_DIM, _VH, LANES), _F32)],
        scratch_shapes=[pltpu.VMEM((5 * A_HEAD_DIM, tb, LANES), _F32),
                        pltpu.VMEM((_VH * tb, LANES), _F32),
                        pltpu.VMEM((tb * _VH, LANES), _F32),
                        pltpu.VMEM((A_HEAD_DIM, _VH, LANES), _F32),
                        pltpu.VMEM((3 * A_HEAD_DIM, SCAN_GROUP * A_HEADS, tb), _F32)],
        compiler_params=_cparams("arbitrary", "arbitrary"), name="rwkv_scan_cm",
    )(*([ops] * SCAN_GROUP))


def _gla_kernel(*refs, nb, chunk, valid, has_s0):
    it = iter(refs)
    p_ref = next(it)
    s0_ref = next(it) if has_s0 else None
    fup_ref = next(it)
    fb_ref = next(it)
    ng_ref = next(it)
    o_ref = next(it)
    sout_ref = next(it)
    s_scr = next(it)

    c_idx = pl.program_id(1)

    @pl.when(c_idx == 0)
    def _():
        if has_s0:
            s_scr[...] = s0_ref[0]
        else:
            s_scr[...] = jnp.zeros_like(s_scr)

    row = lax.broadcasted_iota(jnp.int32, (chunk, chunk), 0)
    col = lax.broadcasted_iota(jnp.int32, (chunk, chunk), 1)
    causal = row >= col
    eye_k = (lax.broadcasted_iota(jnp.int32, (B_DK, B_DK), 0)
             == lax.broadcasted_iota(jnp.int32, (B_DK, B_DK), 1))
    rows = nb * chunk
    trow = lax.broadcasted_iota(jnp.int32, (rows, B_KW), 0) % chunk
    ng = ng_ref[...]

    q_all = p_ref[:, :, 0:B_KW].reshape(rows, B_KW)
    k_all = p_ref[:, :, B_KW:2 * B_KW].reshape(rows, B_KW)
    f_lo = p_ref[:, :, 2 * B_KW + 2 * B_WIDTH:GLA_IN_W].reshape(rows, B_GATE_LORA)
    gk = jax.nn.log_sigmoid(_dot(f_lo, fup_ref[...]) + fb_ref[...]) / GLA_LOGIT_NORM
    if valid < chunk:
        gk = jnp.where(trow < valid, gk, 0.0)
        k_all = jnp.where(trow < valid, k_all, 0.0)
    bc = gk
    s = 1
    while s < chunk:
        bc = bc + jnp.where(trow >= s, pltpu.roll(bc, s, 0), 0.0)
        s *= 2
    b_last3 = bc.reshape(nb, chunk, B_KW)[:, chunk - 1:chunk, :]
    b_last = jnp.broadcast_to(b_last3, (nb, chunk, B_KW)).reshape(rows, B_KW)
    e_last3 = jnp.exp(b_last3)
    qt_all = q_all * (B_DK ** -0.5) * jnp.exp(bc)
    kt_all = k_all * jnp.exp(-bc)
    kd_all = k_all * jnp.exp(b_last - bc)

    units = [(b, h) for b in range(nb) for h in range(B_HEADS)]
    group = 16
    for u0 in range(0, len(units), group):
        grp = units[u0:u0 + group]
        att, qs, kv, vbs, sts = [], [], [], [], []
        for b, h in grp:
            rs = slice(b * chunk, (b + 1) * chunk)
            ks = slice(h * B_DK, (h + 1) * B_DK)
            vs = slice(2 * B_KW + h * B_DV, 2 * B_KW + (h + 1) * B_DV)
            qt = qt_all[rs, ks].astype(_BF16)
            kt = kt_all[rs, ks].astype(_BF16)
            kd = kd_all[rs, ks].astype(_BF16)
            vb = p_ref[b, :, vs].astype(_BF16)
            st = s_scr[b, h]
            att.append(lax.dot_general(qt, kt, (((1,), (1,)), ((), ())),
                                       preferred_element_type=_F32))
            qs.append(jnp.dot(qt, st.astype(_BF16), preferred_element_type=_F32))
            kv.append(lax.dot_general(kd, vb, (((0,), (0,)), ((), ())),
                                      preferred_element_type=_F32))
            vbs.append(vb)
            sts.append(st)
        outs = []
        for i in range(len(grp)):
            a = jnp.where(causal, att[i], 0.0).astype(_BF16)
            outs.append(jnp.dot(a, vbs[i], preferred_element_type=_F32) + qs[i])
        for i, (b, h) in enumerate(grp):
            ks = slice(h * B_DK, (h + 1) * B_DK)
            gs = slice(2 * B_KW + B_WIDTH + h * B_DV, 2 * B_KW + B_WIDTH + (h + 1) * B_DV)
            e_col = jnp.sum(jnp.where(eye_k, jnp.broadcast_to(e_last3[b, :, ks], (B_DK, B_DK)), 0.0),
                            axis=1, keepdims=True)
            s_scr[b, h] = sts[i] * e_col + kv[i]
            o = outs[i]
            gh = p_ref[b, :, gs]
            on = o * lax.rsqrt(jnp.mean(o * o, -1, keepdims=True) + RMS_EPS) * ng
            o_ref[b, :, h * B_DV:(h + 1) * B_DV] = on * (gh * jax.nn.sigmoid(gh))

    @pl.when(c_idx == pl.num_programs(1) - 1)
    def _():
        sout_ref[...] = s_scr[...]


def _gla(pb3, s0_all, layer, fup, fb, ng, *, nb, chunk, valid):
    n_seq, seq_len = pb3.shape[0], pb3.shape[1]
    has_s0 = s0_all is not None
    in_specs = [pl.BlockSpec((nb, chunk, GLA_IN_PAD), lambda i, c: (i, c, 0))]
    args = [pb3]
    if has_s0:
        in_specs.append(pl.BlockSpec((1, nb, B_HEADS, B_DK, B_DV),
                                     lambda i, c: (layer, i, 0, 0, 0)))
        args.append(s0_all)
    in_specs += [pl.BlockSpec((B_GATE_LORA, B_KW), lambda i, c: (0, 0)),
                 pl.BlockSpec((1, B_KW), lambda i, c: (0, 0)),
                 pl.BlockSpec((1, B_DV), lambda i, c: (0, 0))]
    args += [fup, fb, ng]
    return pl.pallas_call(
        functools.partial(_gla_kernel, nb=nb, chunk=chunk, valid=valid, has_s0=has_s0),
        grid=(n_seq // nb, seq_len // chunk),
        in_specs=in_specs,
        out_specs=[pl.BlockSpec((nb, chunk, B_WIDTH), lambda i, c: (i, c, 0)),
                   pl.BlockSpec((nb, B_HEADS, B_DK, B_DV), lambda i, c: (i, 0, 0, 0))],
        out_shape=[jax.ShapeDtypeStruct((n_seq, seq_len, B_WIDTH), _F32),
                   jax.ShapeDtypeStruct((n_seq, B_HEADS, B_DK, B_DV), _F32)],
        scratch_shapes=[pltpu.VMEM((nb, B_HEADS, B_DK, B_DV), _F32)],
        compiler_params=_cparams("parallel", "arbitrary"), name="gla",
    )(*args)


def _outproj_kernel(x_ref, o_ref, gate_ref, bonus_ref, lng_ref, lnb_ref, ob_ref, wa_ref, wb_ref,
                    g_ref, b_ref, y_ref, *, tm):
    o3 = o_ref[...].reshape(A_HEAD_DIM, A_HEADS, tm)
    m = jnp.mean(o3, axis=0, keepdims=True)
    oc = o3 - m
    var = jnp.mean(oc * oc, axis=0, keepdims=True)
    on = (oc * lax.rsqrt(var + GN_EPS_A)).reshape(A_WIDTH, tm)
    oa = ((on * lng_ref[...] + lnb_ref[...] + bonus_ref[...]) * gate_ref[...]).T
    mix = _dot(oa, wa_ref[0]) + _dot(ob_ref[...], wb_ref[0])
    y_ref[...] = _layer_norm(ALPHA * x_ref[...] + mix, g_ref[...], b_ref[...])


def _outproj(x, o_cm, gate, bonus, lng, lnb, ob, wa_all, wb_all, layer, g, b, tm):
    n = x.shape[0]
    rowb = lambda w: pl.BlockSpec((tm, w), lambda i: (i, 0))
    const = lambda r, c: pl.BlockSpec((r, c), lambda i: (0, 0))
    wspec = pl.BlockSpec((1, A_WIDTH, D_MODEL), lambda i: (layer, 0, 0))
    cm = pl.BlockSpec((A_WIDTH, tm), lambda i: (0, i))
    if o_cm.ndim == 3:
        nt = o_cm.shape[2] // tm
        o_spec = pl.BlockSpec((1, A_WIDTH, tm), lambda i: (i // nt, 0, i % nt))
    else:
        o_spec = cm
    return pl.pallas_call(
        functools.partial(_outproj_kernel, tm=tm),
        grid=(n // tm,),
        in_specs=[rowb(D_MODEL), o_spec, cm, cm, const(A_WIDTH, tm), const(A_WIDTH, tm),
                  rowb(B_WIDTH), wspec, wspec, const(1, D_MODEL), const(1, D_MODEL)],
        out_specs=rowb(D_MODEL),
        out_shape=jax.ShapeDtypeStruct((n, D_MODEL), _F32),
        compiler_params=_cparams("parallel"), name="outproj_ln",
    )(x, o_cm, gate, bonus, lng, lnb, ob, wa_all, wb_all, g, b)


def _ffn_kernel(*refs, tm, period):
    it = iter(refs)
    x_ref = next(it)
    past_ref = next(it) if period is not None else None
    wup_ref = next(it)
    cw_ref = next(it)
    cb_ref = next(it)
    wdn_ref = next(it)
    g_ref = next(it)
    b_ref = next(it)
    y_ref = next(it)
    gate_ref = next(it)
    carry_ref = next(it)

    x = x_ref[...]
    u = jnp.dot(x.astype(_BF16), wup_ref[0], preferred_element_type=_F32)
    gate = u[:, :D_FF]
    val = u[:, D_FF:]
    if period is None:
        r1 = pltpu.roll(gate, 1, 0)
        r2 = pltpu.roll(gate, 2, 0)

        @pl.when(pl.program_id(1) == 0)
        def _():
            carry_ref[...] = jnp.zeros_like(carry_ref)
        c = carry_ref[...]
        row8 = lax.broadcasted_iota(jnp.int32, (SUBLANES, D_FF), 0)
        top1 = jnp.where(row8 < 1, pltpu.roll(c, 1, 0), r1[:SUBLANES])
        top2 = jnp.where(row8 < 2, pltpu.roll(c, 2, 0), r2[:SUBLANES])
        g1 = jnp.concatenate([top1, r1[SUBLANES:]], axis=0)
        g2 = jnp.concatenate([top2, r2[SUBLANES:]], axis=0)
        carry_ref[...] = gate[tm - SUBLANES:]
        gate_ref[0] = gate[tm - SUBLANES:]
    else:
        @pl.when(pl.program_id(0) == 0)
        def _():
            carry_ref[...] = past_ref[...]
        g2 = carry_ref[...]
        g1 = jnp.concatenate([g2[period:], gate[:tm - period]], axis=0)
        carry_ref[...] = gate
        gate_ref[...] = gate
    acc = cb_ref[...] + g2 * cw_ref[0:1, :] + g1 * cw_ref[1:2, :] + gate * cw_ref[2:3, :]
    h = 0.5 * acc * (1.0 + lax.erf(acc * (2.0 ** -0.5))) * val
    f = jnp.dot(h.astype(_BF16), wdn_ref[0], preferred_element_type=_F32)
    y_ref[...] = _layer_norm(ALPHA * x + f, g_ref[...], b_ref[...])


def _ffn(x, past, wup_all, cw, cb, wdn_all, layer, g, b, *, n_seq, seq_len, tm):
    n = x.shape[0]
    prompt = past is None
    if prompt:
        nt = seq_len // tm
        grid = (n_seq, nt)
        rmap = lambda bb, t: (bb * nt + t, 0)
        sem = ("parallel", "arbitrary")
    else:
        assert tm == (CONV_W - 1) * n_seq
        grid = (n // tm,)
        rmap = lambda i: (i, 0)
        sem = ("arbitrary",)
    const = lambda r, c: pl.BlockSpec((r, c), lambda *gidx: (0, 0), pipeline_mode=pl.Buffered(1))
    in_specs = [pl.BlockSpec((tm, D_MODEL), rmap)]
    args = [x]
    if not prompt:
        in_specs.append(const(tm, D_FF))
        args.append(past)
    wspec = lambda r, c: pl.BlockSpec((1, r, c), lambda *gidx: (layer, 0, 0),
                                      pipeline_mode=pl.Buffered(1))
    in_specs += [wspec(D_MODEL, 2 * D_FF), const(CONV_W, D_FF), const(1, D_FF),
                 wspec(D_FF, D_MODEL), const(1, D_MODEL), const(1, D_MODEL)]
    args += [wup_all, cw, cb, wdn_all, g, b]
    if prompt:
        gate_spec = pl.BlockSpec((1, SUBLANES, D_FF), lambda bb, t: (bb, 0, 0))
        gate_shape = jax.ShapeDtypeStruct((n_seq, SUBLANES, D_FF), _F32)
        scratch = [pltpu.VMEM((SUBLANES, D_FF), _F32)]
    else:
        gate_spec = pl.BlockSpec((tm, D_FF), rmap)
        gate_shape = jax.ShapeDtypeStruct((n, D_FF), _F32)
        scratch = [pltpu.VMEM((tm, D_FF), _F32)]
    return pl.pallas_call(
        functools.partial(_ffn_kernel, tm=tm, period=None if prompt else n_seq),
        grid=grid, in_specs=in_specs,
        out_specs=[pl.BlockSpec((tm, D_MODEL), rmap), gate_spec],
        out_shape=[jax.ShapeDtypeStruct((n, D_MODEL), _F32), gate_shape],
        scratch_shapes=scratch, compiler_params=_cparams(*sem), name="ffn_ln",
    )(*args)


def _lanes_to_state(s):
    x = s.reshape(A_HEAD_DIM, _VH, 2, SCAN_GROUP, A_HEADS)
    return jnp.transpose(x, (3, 4, 2, 1, 0)).reshape(SCAN_GROUP, A_HEADS, A_HEAD_DIM, A_HEAD_DIM)


def _kh_rows(x):
    return jnp.swapaxes(x.reshape((A_HEADS, A_HEAD_DIM) + x.shape[1:]), 0, 1).reshape(x.shape)


def _rwkv_rows(x):
    blocks = [_kh_rows(x[i * A_WIDTH:(i + 1) * A_WIDTH]) for i in range(3)]
    return jnp.concatenate(blocks + [x[3 * A_WIDTH:]], axis=0)


def _rwkv_rows_inv(x):
    def hk(y):
        return jnp.swapaxes(y.reshape((A_HEAD_DIM, A_HEADS) + y.shape[1:]), 0, 1).reshape(y.shape)
    blocks = [hk(x[i * A_WIDTH:(i + 1) * A_WIDTH]) for i in range(3)]
    return jnp.concatenate(blocks + [x[3 * A_WIDTH:]], axis=0)


def _bcast_cols(p, tb):
    return jnp.broadcast_to(p[:, None], (p.shape[0], tb))


def _stacked_weights(P):
    w_in = P['w_in']
    wta = jax.vmap(_rwkv_rows)(jnp.swapaxes(w_in[:, :, :SHIFT_W], 1, 2)).astype(_BF16)
    wb = jnp.pad(w_in[:, :, SHIFT_W:], ((0, 0), (0, 0), (0, GLA_IN_PAD - GLA_IN_W))).astype(_BF16)
    woa = jax.vmap(_kh_rows)(P['w_out'][:, :A_WIDTH]).astype(_BF16)
    wob = P['w_out'][:, A_WIDTH:].astype(_BF16)
    return dict(wta=wta, wb=wb, woa=woa, wob=wob,
                w_up=P['w_up'].astype(_BF16), w_down=P['w_down'].astype(_BF16))


def _layer_weights(P, l, tb_prep, tb_post):
    vb = _kh_rows(P['vres_bias'][l - 1]) if l > 0 else jnp.zeros((A_WIDTH,), _F32)
    prm = jnp.stack([_kh_rows(P['w0'][l]), _kh_rows(P['a0'][l]), _kh_rows(P['k_k'][l]),
                     _kh_rows(P['k_a'][l]), _kh_rows(P['r_k'][l].reshape(A_WIDTH)), vb])
    prm = jnp.broadcast_to(prm[:, :, None], (6, A_WIDTH, tb_prep))
    if l > 0:
        v1 = _kh_rows(P['vres_down'][l - 1]).T.astype(_BF16)
        v2 = _kh_rows(P['vres_up'][l - 1].T).astype(_BF16)
    else:
        v1 = jnp.zeros((A_MV_LORA, A_WIDTH), _BF16)
        v2 = jnp.zeros((A_WIDTH, A_MV_LORA), _BF16)
    return dict(
        mu=_bcast_cols(_rwkv_rows(P['tok_mu'][l]), tb_prep), prm=prm,
        wup=_kh_rows(P['w_lora_up'][l].T).astype(_BF16),
        aup=_kh_rows(P['a_lora_up'][l].T).astype(_BF16),
        gup=_kh_rows(P['g_lora_up'][l].T).astype(_BF16),
        v1=v1, v2=v2,
        lng=_bcast_cols(_kh_rows(P['lnx_g'][l]), tb_post),
        lnb=_bcast_cols(_kh_rows(P['lnx_b'][l]), tb_post),
        fup=P['gla_f_up'][l].astype(_BF16), fb=P['gla_f_bias'][l][None], ng=P['gla_norm_g'][l][None],
        ln1g=P['ln1_g'][l][None], ln1b=P['ln1_b'][l][None],
        cw=P['conv_w'][l], cb=P['conv_b'][l][None],
        ln2g=P['ln2_g'][l][None], ln2b=P['ln2_b'][l][None])


def _trunk(x3, states, P, SW, prompt):
    n_seq, seq_len, _ = x3.shape
    n = n_seq * seq_len
    tb = 2 * LANES
    if prompt:
        x = x3.reshape(n, D_MODEL)
        scan_tb, gla_chunk, gla_valid, gla_t, tm_out, tm_ffn = LANES, GLA_CHUNK, GLA_CHUNK, seq_len, 256, 256
        st_gla = None
    else:
        assert n_seq == LANES
        x = jnp.swapaxes(x3, 0, 1).reshape(n, D_MODEL)
        st_rwkv, st_shift, st_gla, st_conv = states
        gla_chunk, gla_valid, gla_t, tm_out, tm_ffn = SUBLANES, seq_len, SUBLANES, 256, (CONV_W - 1) * n_seq
    new_rwkv, new_shift, new_gla, new_conv = [], [], [], []
    ops0 = None
    for l in range(DEPTH):
        W = _layer_weights(P, l, tb, tm_out)
        if prompt:
            bnd = None
        else:
            bnd = _rwkv_rows(st_shift[l][:, 0, :].T)
        ops, g, bonus, pb, pa_out = _prep(
            x, SW['wta'], SW['wb'], l, bnd, ops0, W['mu'], W['prm'], W['wup'], W['aup'], W['gup'],
            W['v1'], W['v2'], n_seq=n_seq, seq_len=seq_len, tb=tb)
        if l == 0:
            ops0 = ops
        if prompt:
            o_cm, s_fin = _scan_cm(ops, seq_len, scan_tb)
            s_fin = _lanes_to_state(s_fin)
            last_cols = pa_out[:, :, tb - 1].T
        else:
            o_hv, s_fin = _scan_short(st_rwkv, l, ops, n_seq)
            o_cm = jnp.swapaxes(o_hv, 0, 1).reshape(A_WIDTH, n)
            last_cols = pa_out[:, n - n_seq:]
        if prompt:
            pb3 = pb.reshape(n_seq, seq_len, GLA_IN_PAD)
        else:
            pb3 = jnp.swapaxes(pb.reshape(seq_len, n_seq, GLA_IN_PAD), 0, 1)
            pb3 = jnp.pad(pb3, ((0, 0), (0, gla_t - seq_len), (0, 0)))
        ob, s_gla = _gla(pb3, st_gla, l, W['fup'], W['fb'], W['ng'],
                         nb=SCAN_GROUP, chunk=gla_chunk, valid=gla_valid)
        if prompt:
            ob = ob.reshape(n, B_WIDTH)
        else:
            ob = jnp.swapaxes(ob[:, :seq_len], 0, 1).reshape(n, B_WIDTH)
        x1 = _outproj(x, o_cm, g, bonus, W['lng'], W['lnb'], ob, SW['woa'], SW['wob'], l,
                      W['ln1g'], W['ln1b'], tm_out)
        if prompt:
            past = None
        else:
            past = jnp.swapaxes(st_conv[l], 0, 1).reshape((CONV_W - 1) * n_seq, D_FF)
        x, gate = _ffn(x1, past, SW['w_up'], W['cw'], W['cb'], SW['w_down'], l,
                       W['ln2g'], W['ln2b'], n_seq=n_seq, seq_len=seq_len, tm=tm_ffn)
        new_rwkv.append(s_fin)
        new_shift.append(_rwkv_rows_inv(last_cols).T[:, None, :])
        new_gla.append(s_gla)
        if prompt:
            new_conv.append(gate[:, SUBLANES - (CONV_W - 1):])
        else:
            tail = gate[n - (CONV_W - 1) * n_seq:].reshape(CONV_W - 1, n_seq, D_FF)
            new_conv.append(jnp.swapaxes(tail, 0, 1))
    if prompt:
        y = x.reshape(n_seq, seq_len, D_MODEL)
    else:
        y = jnp.swapaxes(x.reshape(seq_len, n_seq, D_MODEL), 0, 1)
    return (y, jnp.stack(new_rwkv), jnp.stack(new_shift), jnp.stack(new_gla), jnp.stack(new_conv))


def kernel(x_prompt, x_sample, state_rwkv, state_shift, state_gla, state_conv, w_in, tok_mu, w0, w_lora_up, a0, a_lora_up, g_lora_up, k_k, k_a, r_k, lnx_g, lnx_b, vres_bias, vres_down, vres_up, gla_f_up, gla_f_bias, gla_norm_g, w_out, ln1_g, ln1_b, w_up, conv_w, conv_b, w_down, ln2_g, ln2_b):
    P = dict(w_in=w_in, tok_mu=tok_mu, w0=w0, w_lora_up=w_lora_up, a0=a0, a_lora_up=a_lora_up,
             g_lora_up=g_lora_up, k_k=k_k, k_a=k_a, r_k=r_k, lnx_g=lnx_g, lnx_b=lnx_b,
             vres_bias=vres_bias, vres_down=vres_down, vres_up=vres_up, gla_f_up=gla_f_up,
             gla_f_bias=gla_f_bias, gla_norm_g=gla_norm_g, w_out=w_out, ln1_g=ln1_g, ln1_b=ln1_b,
             w_up=w_up, conv_w=conv_w, conv_b=conv_b, w_down=w_down, ln2_g=ln2_g, ln2_b=ln2_b)
    SW = _stacked_weights(P)
    y_p, rwkv_p, shift_p, gla_p, conv_p = _trunk(x_prompt, None, P, SW, True)
    y_s, rwkv_s, shift_s, gla_s, conv_s = _trunk(
        x_sample, (state_rwkv, state_shift, state_gla, state_conv), P, SW, False)
    return (y_p, y_s, rwkv_p, rwkv_s, shift_p, shift_s, gla_p, gla_s, conv_p, conv_s)
```

```python
import functools

import jax
import jax.numpy as jnp
from jax import lax
from jax.experimental import pallas as pl
from jax.experimental.pallas import tpu as pltpu

D_MODEL = 1024
DEPTH = 4
A_WIDTH = 512
A_HEAD_DIM = 64
A_HEADS = 8
A_DECAY_LORA = 64
A_AAA_LORA = 64
A_MV_LORA = 32
A_GATE_LORA = 128
GN_EPS_A = 64e-5
B_WIDTH = 512
B_HEADS = 4
B_DV = 128
B_DK = 64
B_KW = 256
B_GATE_LORA = 16
GLA_LOGIT_NORM = 16.0
GLA_CHUNK = 64
RMS_EPS = 1e-5
D_FF = 2816
CONV_W = 3
ALPHA = (2 * DEPTH) ** 0.25
LN_EPS = 1e-5
SHIFT_W = 3 * A_WIDTH + A_DECAY_LORA + A_AAA_LORA + A_GATE_LORA
GLA_IN_W = 2 * B_KW + 2 * B_WIDTH + B_GATE_LORA
GLA_IN_PAD = 1664

LANES = 128
SUBLANES = 8
SCAN_GROUP = 8
VMEM_LIMIT = 56 * 1024 * 1024


_F32 = jnp.float32
_BF16 = jnp.bfloat16


def _cparams(*sem):
    return pltpu.CompilerParams(dimension_semantics=sem, vmem_limit_bytes=VMEM_LIMIT)


def _dot(a, b):
    return jnp.dot(a.astype(_BF16), b.astype(_BF16), preferred_element_type=_F32)


def _layer_norm(y, g, b):
    mu = jnp.mean(y, -1, keepdims=True)
    yc = y - mu
    var = jnp.mean(yc * yc, -1, keepdims=True)
    return yc * lax.rsqrt(var + LN_EPS) * g + b


def _head_sum(x, tb):
    return jnp.sum(x.reshape(A_HEAD_DIM, A_HEADS, tb), axis=0)


def _head_bcast(s, tb):
    return jnp.broadcast_to(s[None], (A_HEAD_DIM, A_HEADS, tb)).reshape(A_WIDTH, tb)


def _prep_kernel(*refs, tb, period, has_vmix):
    it = iter(refs)
    x_ref = next(it)
    wta_ref = next(it)
    wb_ref = next(it)
    bnd_ref = next(it) if period is not None else None
    vfirst_ref = next(it) if has_vmix else None
    mu_ref = next(it)
    prm_ref = next(it)
    wup_ref = next(it)
    aup_ref = next(it)
    gup_ref = next(it)
    v1_ref = next(it)
    v2_ref = next(it)
    ops_ref = next(it)
    g_ref = next(it)
    bonus_ref = next(it)
    pb_ref = next(it)
    pa_out_ref = next(it)
    carry_ref = next(it)

    x = x_ref[...].astype(_BF16)
    pb_ref[...] = jnp.dot(x, wb_ref[0], preferred_element_type=_F32)
    pa = lax.dot_general(wta_ref[0], x, (((1,), (1,)), ((), ())), preferred_element_type=_F32)
    if period is None:
        lane = lax.broadcasted_iota(jnp.int32, (SHIFT_W, tb), 1)

        @pl.when(pl.program_id(1) == 0)
        def _():
            carry_ref[...] = jnp.zeros_like(carry_ref)
        prev = jnp.where(lane == 0, pltpu.roll(carry_ref[...], 1, 1), pltpu.roll(pa, 1, 1))
        carry_ref[...] = pa
        pa_out_ref[0] = pa
    else:
        @pl.when(pl.program_id(0) == 0)
        def _():
            carry_ref[...] = bnd_ref[...]
        prev = jnp.concatenate([carry_ref[...], pa[:, :tb - period]], axis=1)
        carry_ref[...] = pa[:, tb - period:]
        pa_out_ref[...] = pa

    xs = pa + (prev - pa) * mu_ref[...]
    r = xs[0:A_WIDTH]
    k = xs[A_WIDTH:2 * A_WIDTH]
    v = xs[2 * A_WIDTH:3 * A_WIDTH]
    o0 = 3 * A_WIDTH
    w_lo = xs[o0:o0 + A_DECAY_LORA]
    a_lo = xs[o0 + A_DECAY_LORA:o0 + A_DECAY_LORA + A_AAA_LORA]
    g_lo = xs[o0 + A_DECAY_LORA + A_AAA_LORA:SHIFT_W]

    w0, a0, k_k, k_a, r_k = prm_ref[0], prm_ref[1], prm_ref[2], prm_ref[3], prm_ref[4]
    w = -jax.nn.softplus(-(w0 + _dot(wup_ref[...], jnp.tanh(w_lo)))) - 0.5
    log_decay = -jnp.exp(w)
    decay = log_decay if period is None else jnp.exp(log_decay)
    a = jax.nn.sigmoid(a0 + _dot(aup_ref[...], a_lo))
    g = _dot(gup_ref[...], jax.nn.sigmoid(g_lo))
    if has_vmix:
        mix = jax.nn.sigmoid(prm_ref[5] + _dot(v2_ref[...], _dot(v1_ref[...], v)))
        v = v + (vfirst_ref[0] - v) * mix
    kk = k * k_k
    ss = _head_sum(kk * kk, tb)
    kk = kk * _head_bcast(lax.rsqrt(jnp.maximum(ss, 1e-24)), tb)
    kx = k * (1.0 + (a - 1.0) * k_a)
    ops_ref[0] = r
    ops_ref[1] = kk
    ops_ref[2] = decay
    ops_ref[3] = kk * a
    ops_ref[4] = kx
    ops_ref[5] = v
    g_ref[...] = g
    bonus_ref[...] = _head_bcast(_head_sum(r * kx * r_k, tb), tb) * v


def _prep(x, wta_all, wb_all, layer, bnd, vfirst_ops, mu, prm, wup, aup, gup, v1, v2, *,
          n_seq, seq_len, tb):
    n = x.shape[0]
    prompt = bnd is None
    has_vmix = vfirst_ops is not None
    if prompt:
        nt = seq_len // tb
        grid = (n_seq, nt)
        tok = lambda b, t: b * nt + t
        sem = ("parallel", "arbitrary")
    else:
        assert tb % n_seq == 0 and tb > n_seq
        grid = (n // tb,)
        tok = lambda i: i
        sem = ("arbitrary",)
    cmap = lambda f: (lambda *g: f(tok(*g)))
    in_specs = [pl.BlockSpec((tb, D_MODEL), cmap(lambda j: (j, 0))),
                pl.BlockSpec((1, SHIFT_W, D_MODEL), lambda *g: (layer, 0, 0)),
                pl.BlockSpec((1, D_MODEL, GLA_IN_PAD), lambda *g: (layer, 0, 0))]
    args = [x, wta_all, wb_all]
    if not prompt:
        in_specs.append(pl.BlockSpec((SHIFT_W, n_seq), lambda i: (0, 0)))
        args.append(bnd)
    if has_vmix:
        in_specs.append(pl.BlockSpec((1, A_WIDTH, tb), cmap(lambda j: (5, 0, j))))
        args.append(vfirst_ops)
    const2 = lambda *g: (0, 0)
    const3 = lambda *g: (0, 0, 0)
    in_specs += [pl.BlockSpec((SHIFT_W, tb), const2),
                 pl.BlockSpec((6, A_WIDTH, tb), const3),
                 pl.BlockSpec((A_WIDTH, A_DECAY_LORA), const2),
                 pl.BlockSpec((A_WIDTH, A_AAA_LORA), const2),
                 pl.BlockSpec((A_WIDTH, A_GATE_LORA), const2),
                 pl.BlockSpec((A_MV_LORA, A_WIDTH), const2),
                 pl.BlockSpec((A_WIDTH, A_MV_LORA), const2)]
    args += [mu, prm, wup, aup, gup, v1, v2]
    out_specs = [pl.BlockSpec((6, A_WIDTH, tb), cmap(lambda j: (0, 0, j))),
                 pl.BlockSpec((A_WIDTH, tb), cmap(lambda j: (0, j))),
                 pl.BlockSpec((A_WIDTH, tb), cmap(lambda j: (0, j))),
                 pl.BlockSpec((tb, GLA_IN_PAD), cmap(lambda j: (j, 0)))]
    out_shape = [jax.ShapeDtypeStruct((6, A_WIDTH, n), _F32),
                 jax.ShapeDtypeStruct((A_WIDTH, n), _F32),
                 jax.ShapeDtypeStruct((A_WIDTH, n), _F32),
                 jax.ShapeDtypeStruct((n, GLA_IN_PAD), _F32)]
    scratch = [pltpu.VMEM((SHIFT_W, tb if prompt else n_seq), _F32)]
    if prompt:
        out_specs.append(pl.BlockSpec((1, SHIFT_W, tb), lambda b, t: (b, 0, 0)))
        out_shape.append(jax.ShapeDtypeStruct((n_seq, SHIFT_W, tb), _F32))
    else:
        out_specs.append(pl.BlockSpec((SHIFT_W, tb), cmap(lambda j: (0, j))))
        out_shape.append(jax.ShapeDtypeStruct((SHIFT_W, n), _F32))
    return pl.pallas_call(
        functools.partial(_prep_kernel, tb=tb, period=None if prompt else n_seq,
                          has_vmix=has_vmix),
        grid=grid, in_specs=in_specs, out_specs=out_specs, out_shape=out_shape,
        scratch_shapes=scratch, compiler_params=_cparams(*sem), name="rwkv_prep",
    )(*args)


_VH = A_HEAD_DIM // 2


_OP_R, _OP_KK, _OP_DECAY, _OP_B, _OP_K, _OP_V = range(6)


def _scan_short_kernel(s_ref, *refs, n_steps, n_seq):
    step_refs = refs[:n_steps]
    o_ref, sout_ref, s_scr = refs[n_steps:]
    head = pl.program_id(0)
    for v in range(0, A_HEAD_DIM, 2):
        pair = [s_ref[0, :, 0, v + j, :] for j in range(2)]
        s_scr[pl.ds(v * A_HEAD_DIM, 2 * A_HEAD_DIM), :] = jnp.concatenate(pair, axis=1).T

    for t in range(n_steps):
        lanes = slice(t * n_seq, (t + 1) * n_seq)
        ops_ref = step_refs[t]

        def operand(op):
            return ops_ref[op, pl.ds(head, A_HEAD_DIM, stride=A_HEADS), :]

        r, kk, decay, b, kx = (operand(op) for op in (_OP_R, _OP_KK, _OP_DECAY, _OP_B, _OP_K))
        for v in range(A_HEAD_DIM):
            rows = pl.ds(v * A_HEAD_DIM, A_HEAD_DIM)
            s = s_scr[rows, :]
            sa = -jnp.sum(s * kk, axis=0, keepdims=True)
            vrow = ops_ref[_OP_V, pl.ds(v * A_HEADS + head, 1), :]
            s_new = s * decay + sa * b + vrow * kx
            s_scr[rows, :] = s_new
            o_ref[0, pl.ds(v, 1), lanes] = jnp.sum(s_new * r, axis=0, keepdims=True)

    for v in range(0, A_HEAD_DIM, 2):
        pair = s_scr[pl.ds(v * A_HEAD_DIM, 2 * A_HEAD_DIM), :].T
        for j in range(2):
            sout_ref[:, 0, v + j, :] = pair[:, j * A_HEAD_DIM:(j + 1) * A_HEAD_DIM]


def _scan_short(s_all, layer, ops, n_seq):
    n = ops.shape[2]
    n_steps = n // n_seq
    s_block = (n_seq, 1, A_HEAD_DIM, A_HEAD_DIM)

    def step_spec(t):
        return pl.BlockSpec((6, A_WIDTH, n_seq), lambda h: (0, 0, t))

    return pl.pallas_call(
        functools.partial(_scan_short_kernel, n_steps=n_steps, n_seq=n_seq),
        grid=(A_HEADS,),
        in_specs=[pl.BlockSpec((1,) + s_block, lambda h: (layer, 0, h, 0, 0))]
        + [step_spec(t) for t in range(n_steps)],
        out_specs=[pl.BlockSpec((1, A_HEAD_DIM, n), lambda h: (h, 0, 0)),
                   pl.BlockSpec(s_block, lambda h: (0, h, 0, 0))],
        out_shape=[jax.ShapeDtypeStruct((A_HEADS, A_HEAD_DIM, n), _F32),
                   jax.ShapeDtypeStruct((n_seq, A_HEADS, A_HEAD_DIM, A_HEAD_DIM), _F32)],
        scratch_shapes=[pltpu.VMEM((A_HEAD_DIM * A_HEAD_DIM, n_seq), _F32)],
        compiler_params=_cparams("parallel"), name="rwkv_scan_short",
    )(s_all, *([ops] * n_steps))


_SCAN_WIN = 32
_PHASE_OPS = (_OP_DECAY, _OP_R, _OP_KK, _OP_B, _OP_K, _OP_V)


def _scan_cm_kernel(*refs, tb):
    in_refs = refs[:SCAN_GROUP]
    o_ref, sout_ref, z_scr, zv_scr, o_scr, s_scr, d_scr = refs[SCAN_GROUP:]
    t_blk = pl.program_id(0)
    phase = pl.program_id(1)
    n_win = tb // _SCAN_WIN

    @pl.when(jnp.logical_and(t_blk == 0, phase == 0))
    def _():
        s_scr[...] = jnp.zeros_like(s_scr)

    def gather(k):
        rows = slice(k * A_HEADS, (k + 1) * A_HEADS)
        return jnp.concatenate([r[0, rows, :] for r in in_refs], axis=0)

    @pl.when(phase == 0)
    def _():
        kg = 8
        nrow = SCAN_GROUP * A_HEADS
        lane = lax.broadcasted_iota(jnp.int32, (kg * nrow, tb), 1) % _SCAN_WIN
        for k0 in range(0, A_HEAD_DIM, kg):
            logd = jnp.concatenate([gather(k0 + j) for j in range(kg)], axis=0)
            c = logd
            s = 1
            while s < _SCAN_WIN:
                c = c + jnp.where(lane >= s, pltpu.roll(c, s, 1), 0.0)
                s *= 2
            dm = jnp.exp(c)
            di = jnp.exp(-c)
            dp = jnp.exp(c - logd)
            for j in range(kg):
                rows = slice(j * nrow, (j + 1) * nrow)
                d_scr[k0 + j] = dm[rows]
                d_scr[A_HEAD_DIM + k0 + j] = di[rows]
                d_scr[2 * A_HEAD_DIM + k0 + j] = dp[rows]
                z_scr[k0 + j] = jnp.concatenate([dm[rows], dm[rows]], axis=0).T

    @pl.when(jnp.logical_and(phase >= 1, phase <= 4))
    def _():
        fac = jnp.where(phase == 1, 0, jnp.where(phase == 2, 2, 1)) * A_HEAD_DIM
        for k in range(A_HEAD_DIM):
            m = gather(k) * d_scr[fac + k]
            z_scr[phase * A_HEAD_DIM + k] = jnp.concatenate([m, m], axis=0).T

    @pl.when(phase == 5)
    def _():
        for v in range(_VH):
            lo = slice(v * A_HEADS, (v + 1) * A_HEADS)
            hi = slice((_VH + v) * A_HEADS, (_VH + v + 1) * A_HEADS)
            m = jnp.concatenate([r[0, lo, :] for r in in_refs]
                                + [r[0, hi, :] for r in in_refs], axis=0)
            zv_scr[pl.ds(v * tb, tb), :] = m.T

    @pl.when(phase == 6)
    def _():
        def row(t, slot, k):
            return z_scr[slot * A_HEAD_DIM + k, pl.ds(t, 1), :]

        zero = jnp.zeros((_VH, LANES), _F32)

        def window(w, carry):
            t0 = w * _SCAN_WIN
            acc = [zero, zero]
            for k in range(A_HEAD_DIM):
                acc[k % 2] = acc[k % 2] + s_scr[k] * row(t0, 2, k)

            def step(i, nacc):
                t = t0 + i
                tn = jnp.minimum(t + 1, tb - 1)
                sa = -nacc
                vv = zv_scr[pl.ds(t, _VH, stride=tb), :]
                out = [zero, zero]
                nxt = [zero, zero]
                for k in range(A_HEAD_DIM):
                    s_new = s_scr[k] + sa * row(t, 3, k) + vv * row(t, 4, k)
                    s_scr[k] = s_new
                    out[k % 2] = out[k % 2] + s_new * row(t, 1, k)
                    nxt[k % 2] = nxt[k % 2] + s_new * row(tn, 2, k)
                o_scr[pl.ds(pl.multiple_of(t * _VH, _VH), _VH), :] = out[0] + out[1]
                return nxt[0] + nxt[1]

            lax.fori_loop(0, _SCAN_WIN, step, acc[0] + acc[1])
            t_end = t0 + _SCAN_WIN - 1
            for k in range(A_HEAD_DIM):
                s_scr[k] = s_scr[k] * row(t_end, 0, k)
            return carry

        lax.fori_loop(0, n_win, window, 0)
        for v in range(_VH):
            mt = o_scr[pl.ds(v, tb, stride=_VH), :].T
            for half in range(2):
                for b in range(SCAN_GROUP):
                    src = (half * SCAN_GROUP + b) * A_HEADS
                    dst = (half * _VH + v) * A_HEADS
                    o_ref[b, dst:dst + A_HEADS, :] = mt[src:src + A_HEADS, :]
        sout_ref[...] = s_scr[...]


def _scan_cm(ops, seq_len, tb):
    nt = seq_len // tb
    n_phase = len(_PHASE_OPS) + 1

    def op_of_phase(p):
        op = _PHASE_OPS[-1]
        for i in range(len(_PHASE_OPS) - 2, -1, -1):
            op = jnp.where(p == i, _PHASE_OPS[i], op)
        return op

    def in_spec(b):
        return pl.BlockSpec((1, A_WIDTH, tb), lambda t, p: (op_of_phase(p), 0, b * nt + t))

    return pl.pallas_call(
        functools.partial(_scan_cm_kernel, tb=tb),
        grid=(nt, n_phase),
        in_specs=[in_spec(b) for b in range(SCAN_GROUP)],
        out_specs=[pl.BlockSpec((SCAN_GROUP, A_WIDTH, tb), lambda t, p: (0, 0, t)),
                   pl.BlockSpec((A_HEAD_DIM, _VH, LANES), lambda t, p: (0, 0, 0))],
        out_shape=[jax.ShapeDtypeStruct((SCAN_GROUP, A_WIDTH, seq_len), _F32),
                   jax.ShapeDtypeStruct((A_HEAD_DIM, _VH, LANES), _F32)],
        scratch_shapes=[pltpu.VMEM((5 * A_HEAD_DIM, tb, LANES), _F32),
                        pltpu.VMEM((_VH * tb, LANES), _F32),
                        pltpu.VMEM((tb * _VH, LANES), _F32),
                        pltpu.VMEM((A_HEAD_DIM, _VH, LANES), _F32),
                        pltpu.VMEM((3 * A_HEAD_DIM, SCAN_GROUP * A_HEADS, tb), _F32)],
        compiler_params=_cparams("arbitrary", "arbitrary"), name="rwkv_scan_cm",
    )(*([ops] * SCAN_GROUP))


def _gla_kernel(*refs, nb, chunk, valid, has_s0):
    it = iter(refs)
    p_ref = next(it)
    s0_ref = next(it) if has_s0 else None
    fup_ref = next(it)
    fb_ref = next(it)
    ng_ref = next(it)
    o_ref = next(it)
    sout_ref = next(it)
    s_scr = next(it)

    c_idx = pl.program_id(1)

    @pl.when(c_idx == 0)
    def _():
        if has_s0:
            s_scr[...] = s0_ref[0]
        else:
            s_scr[...] = jnp.zeros_like(s_scr)

    row = lax.broadcasted_iota(jnp.int32, (chunk, chunk), 0)
    col = lax.broadcasted_iota(jnp.int32, (chunk, chunk), 1)
    causal = row >= col
    eye_k = (lax.broadcasted_iota(jnp.int32, (B_DK, B_DK), 0)
             == lax.broadcasted_iota(jnp.int32, (B_DK, B_DK), 1))
    rows = nb * chunk
    trow = lax.broadcasted_iota(jnp.int32, (rows, B_KW), 0) % chunk
    ng = ng_ref[...]

    q_all = p_ref[:, :, 0:B_KW].reshape(rows, B_KW)
    k_all = p_ref[:, :, B_KW:2 * B_KW].reshape(rows, B_KW)
    f_lo = p_ref[:, :, 2 * B_KW + 2 * B_WIDTH:GLA_IN_W].reshape(rows, B_GATE_LORA)
    gk = jax.nn.log_sigmoid(_dot(f_lo, fup_ref[...]) + fb_ref[...]) / GLA_LOGIT_NORM
    if valid < chunk:
        gk = jnp.where(trow < valid, gk, 0.0)
        k_all = jnp.where(trow < valid, k_all, 0.0)
    bc = gk
    s = 1
    while s < chunk:
        bc = bc + jnp.where(trow >= s, pltpu.roll(bc, s, 0), 0.0)
        s *= 2
    b_last3 = bc.reshape(nb, chunk, B_KW)[:, chunk - 1:chunk, :]
    b_last = jnp.broadcast_to(b_last3, (nb, chunk, B_KW)).reshape(rows, B_KW)
    e_last3 = jnp.exp(b_last3)
    qt_all = q_all * (B_DK ** -0.5) * jnp.exp(bc)
    kt_all = k_all * jnp.exp(-bc)
    kd_all = k_all * jnp.exp(b_last - bc)

    units = [(b, h) for b in range(nb) for h in range(B_HEADS)]
    group = 32
    for u0 in range(0, len(units), group):
        grp = units[u0:u0 + group]
        att, qs, kv, vbs, sts = [], [], [], [], []
        for b, h in grp:
            rs = slice(b * chunk, (b + 1) * chunk)
            ks = slice(h * B_DK, (h + 1) * B_DK)
            vs = slice(2 * B_KW + h * B_DV, 2 * B_KW + (h + 1) * B_DV)
            qt = qt_all[rs, ks].astype(_BF16)
            kt = kt_all[rs, ks].astype(_BF16)
            kd = kd_all[rs, ks].astype(_BF16)
            vb = p_ref[b, :, vs].astype(_BF16)
            st = s_scr[b, h]
            att.append(lax.dot_general(qt, kt, (((1,), (1,)), ((), ())),
                                       preferred_element_type=_F32))
            qs.append(jnp.dot(qt, st.astype(_BF16), preferred_element_type=_F32))
            kv.append(lax.dot_general(kd, vb, (((0,), (0,)), ((), ())),
                                      preferred_element_type=_F32))
            vbs.append(vb)
            sts.append(st)
        outs = []
        for i in range(len(grp)):
            a = jnp.where(causal, att[i], 0.0).astype(_BF16)
            outs.append(jnp.dot(a, vbs[i], preferred_element_type=_F32) + qs[i])
        for i, (b, h) in enumerate(grp):
            ks = slice(h * B_DK, (h + 1) * B_DK)
            gs = slice(2 * B_KW + B_WIDTH + h * B_DV, 2 * B_KW + B_WIDTH + (h + 1) * B_DV)
            e_col = jnp.sum(jnp.where(eye_k, jnp.broadcast_to(e_last3[b, :, ks], (B_DK, B_DK)), 0.0),
                            axis=1, keepdims=True)
            s_scr[b, h] = sts[i] * e_col + kv[i]
            o = outs[i]
            gh = p_ref[b, :, gs]
            on = o * lax.rsqrt(jnp.mean(o * o, -1, keepdims=True) + RMS_EPS) * ng
            o_ref[b, :, h * B_DV:(h + 1) * B_DV] = on * (gh * jax.nn.sigmoid(gh))

    @pl.when(c_idx == pl.num_programs(1) - 1)
    def _():
        sout_ref[...] = s_scr[...]


def _gla(pb3, s0_all, layer, fup, fb, ng, *, nb, chunk, valid):
    n_seq, seq_len = pb3.shape[0], pb3.shape[1]
    has_s0 = s0_all is not None
    in_specs = [pl.BlockSpec((nb, chunk, GLA_IN_PAD), lambda i, c: (i, c, 0))]
    args = [pb3]
    if has_s0:
        in_specs.append(pl.BlockSpec((1, nb, B_HEADS, B_DK, B_DV),
                                     lambda i, c: (layer, i, 0, 0, 0)))
        args.append(s0_all)
    in_specs += [pl.BlockSpec((B_GATE_LORA, B_KW), lambda i, c: (0, 0)),
                 pl.BlockSpec((1, B_KW), lambda i, c: (0, 0)),
                 pl.BlockSpec((1, B_DV), lambda i, c: (0, 0))]
    args += [fup, fb, ng]
    return pl.pallas_call(
        functools.partial(_gla_kernel, nb=nb, chunk=chunk, valid=valid, has_s0=has_s0),
        grid=(n_seq // nb, seq_len // chunk),
        in_specs=in_specs,
        out_specs=[pl.BlockSpec((nb, chunk, B_WIDTH), lambda i, c: (i, c, 0)),
                   pl.BlockSpec((nb, B_HEADS, B_DK, B_DV), lambda i, c: (i, 0, 0, 0))],
        out_shape=[jax.ShapeDtypeStruct((n_seq, seq_len, B_WIDTH), _F32),
                   jax.ShapeDtypeStruct((n_seq, B_HEADS, B_DK, B_DV), _F32)],
        scratch_shapes=[pltpu.VMEM((nb, B_HEADS, B_DK, B_DV), _F32)],
        compiler_params=_cparams("parallel", "arbitrary"), name="gla",
    )(*args)


def _outproj_kernel(x_ref, o_ref, gate_ref, bonus_ref, lng_ref, lnb_ref, ob_ref, wa_ref, wb_ref,
                    g_ref, b_ref, y_ref, *, tm):
    o3 = o_ref[...].reshape(A_HEAD_DIM, A_HEADS, tm)
    m = jnp.mean(o3, axis=0, keepdims=True)
    oc = o3 - m
    var = jnp.mean(oc * oc, axis=0, keepdims=True)
    on = (oc * lax.rsqrt(var + GN_EPS_A)).reshape(A_WIDTH, tm)
    oa = ((on * lng_ref[...] + lnb_ref[...] + bonus_ref[...]) * gate_ref[...]).T
    mix = _dot(oa, wa_ref[0]) + _dot(ob_ref[...], wb_ref[0])
    y_ref[...] = _layer_norm(ALPHA * x_ref[...] + mix, g_ref[...], b_ref[...])


def _outproj(x, o_cm, gate, bonus, lng, lnb, ob, wa_all, wb_all, layer, g, b, tm):
    n = x.shape[0]
    rowb = lambda w: pl.BlockSpec((tm, w), lambda i: (i, 0))
    const = lambda r, c: pl.BlockSpec((r, c), lambda i: (0, 0))
    wspec = pl.BlockSpec((1, A_WIDTH, D_MODEL), lambda i: (layer, 0, 0))
    cm = pl.BlockSpec((A_WIDTH, tm), lambda i: (0, i))
    if o_cm.ndim == 3:
        nt = o_cm.shape[2] // tm
        o_spec = pl.BlockSpec((1, A_WIDTH, tm), lambda i: (i // nt, 0, i % nt))
    else:
        o_spec = cm
    return pl.pallas_call(
        functools.partial(_outproj_kernel, tm=tm),
        grid=(n // tm,),
        in_specs=[rowb(D_MODEL), o_spec, cm, cm, const(A_WIDTH, tm), const(A_WIDTH, tm),
                  rowb(B_WIDTH), wspec, wspec, const(1, D_MODEL), const(1, D_MODEL)],
        out_specs=rowb(D_MODEL),
        out_shape=jax.ShapeDtypeStruct((n, D_MODEL), _F32),
        compiler_params=_cparams("parallel"), name="outproj_ln",
    )(x, o_cm, gate, bonus, lng, lnb, ob, wa_all, wb_all, g, b)


def _ffn_kernel(*refs, tm, period):
    it = iter(refs)
    x_ref = next(it)
    past_ref = next(it) if period is not None else None
    wup_ref = next(it)
    cw_ref = next(it)
    cb_ref = next(it)
    wdn_ref = next(it)
    g_ref = next(it)
    b_ref = next(it)
    y_ref = next(it)
    gate_ref = next(it)
    carry_ref = next(it)

    x = x_ref[...]
    u = jnp.dot(x.astype(_BF16), wup_ref[0], preferred_element_type=_F32)
    gate = u[:, :D_FF]
    val = u[:, D_FF:]
    if period is None:
        r1 = pltpu.roll(gate, 1, 0)
        r2 = pltpu.roll(gate, 2, 0)

        @pl.when(pl.program_id(1) == 0)
        def _():
            carry_ref[...] = jnp.zeros_like(carry_ref)
        c = carry_ref[...]
        row8 = lax.broadcasted_iota(jnp.int32, (SUBLANES, D_FF), 0)
        top1 = jnp.where(row8 < 1, pltpu.roll(c, 1, 0), r1[:SUBLANES])
        top2 = jnp.where(row8 < 2, pltpu.roll(c, 2, 0), r2[:SUBLANES])
        g1 = jnp.concatenate([top1, r1[SUBLANES:]], axis=0)
        g2 = jnp.concatenate([top2, r2[SUBLANES:]], axis=0)
        carry_ref[...] = gate[tm - SUBLANES:]
        gate_ref[0] = gate[tm - SUBLANES:]
    else:
        @pl.when(pl.program_id(0) == 0)
        def _():
            carry_ref[...] = past_ref[...]
        g2 = carry_ref[...]
        g1 = jnp.concatenate([g2[period:], gate[:tm - period]], axis=0)
        carry_ref[...] = gate
        gate_ref[...] = gate
    acc = cb_ref[...] + g2 * cw_ref[0:1, :] + g1 * cw_ref[1:2, :] + gate * cw_ref[2:3, :]
    h = 0.5 * acc * (1.0 + lax.erf(acc * (2.0 ** -0.5))) * val
    f = jnp.dot(h.astype(_BF16), wdn_ref[0], preferred_element_type=_F32)
    y_ref[...] = _layer_norm(ALPHA * x + f, g_ref[...], b_ref[...])


def _ffn(x, past, wup_all, cw, cb, wdn_all, layer, g, b, *, n_seq, seq_len, tm):
    n = x.shape[0]
    prompt = past is None
    if prompt:
        nt = seq_len // tm
        grid = (n_seq, nt)
        rmap = lambda bb, t: (bb * nt + t, 0)
        sem = ("parallel", "arbitrary")
    else:
        assert tm == (CONV_W - 1) * n_seq
        grid = (n // tm,)
        rmap = lambda i: (i, 0)
        sem = ("arbitrary",)
    const = lambda r, c: pl.BlockSpec((r, c), lambda *gidx: (0, 0), pipeline_mode=pl.Buffered(1))
    in_specs = [pl.BlockSpec((tm, D_MODEL), rmap)]
    args = [x]
    if not prompt:
        in_specs.append(const(tm, D_FF))
        args.append(past)
    wspec = lambda r, c: pl.BlockSpec((1, r, c), lambda *gidx: (layer, 0, 0),
                                      pipeline_mode=pl.Buffered(1))
    in_specs += [wspec(D_MODEL, 2 * D_FF), const(CONV_W, D_FF), const(1, D_FF),
                 wspec(D_FF, D_MODEL), const(1, D_MODEL), const(1, D_MODEL)]
    args += [wup_all, cw, cb, wdn_all, g, b]
    if prompt:
        gate_spec = pl.BlockSpec((1, SUBLANES, D_FF), lambda bb, t: (bb, 0, 0))
        gate_shape = jax.ShapeDtypeStruct((n_seq, SUBLANES, D_FF), _F32)
        scratch = [pltpu.VMEM((SUBLANES, D_FF), _F32)]
    else:
        gate_spec = pl.BlockSpec((tm, D_FF), rmap)
        gate_shape = jax.ShapeDtypeStruct((n, D_FF), _F32)
        scratch = [pltpu.VMEM((tm, D_FF), _F32)]
    return pl.pallas_call(
        functools.partial(_ffn_kernel, tm=tm, period=None if prompt else n_seq),
        grid=grid, in_specs=in_specs,
        out_specs=[pl.BlockSpec((tm, D_MODEL), rmap), gate_spec],
        out_shape=[jax.ShapeDtypeStruct((n, D_MODEL), _F32), gate_shape],
        scratch_shapes=scratch, compiler_params=_cparams(*sem), name="ffn_ln",
    )(*args)


def _lanes_to_state(s):
    x = s.reshape(A_HEAD_DIM, _VH, 2, SCAN_GROUP, A_HEADS)
    return jnp.transpose(x, (3, 4, 2, 1, 0)).reshape(SCAN_GROUP, A_HEADS, A_HEAD_DIM, A_HEAD_DIM)


def _kh_rows(x):
    return jnp.swapaxes(x.reshape((A_HEADS, A_HEAD_DIM) + x.shape[1:]), 0, 1).reshape(x.shape)


def _rwkv_rows(x):
    blocks = [_kh_rows(x[i * A_WIDTH:(i + 1) * A_WIDTH]) for i in range(3)]
    return jnp.concatenate(blocks + [x[3 * A_WIDTH:]], axis=0)


def _rwkv_rows_inv(x):
    def hk(y):
        return jnp.swapaxes(y.reshape((A_HEAD_DIM, A_HEADS) + y.shape[1:]), 0, 1).reshape(y.shape)
    blocks = [hk(x[i * A_WIDTH:(i + 1) * A_WIDTH]) for i in range(3)]
    return jnp.concatenate(blocks + [x[3 * A_WIDTH:]], axis=0)


def _bcast_cols(p, tb):
    return jnp.broadcast_to(p[:, None], (p.shape[0], tb))


def _stacked_weights(P):
    w_in = P['w_in']
    wta = jax.vmap(_rwkv_rows)(jnp.swapaxes(w_in[:, :, :SHIFT_W], 1, 2)).astype(_BF16)
    wb = jnp.pad(w_in[:, :, SHIFT_W:], ((0, 0), (0, 0), (0, GLA_IN_PAD - GLA_IN_W))).astype(_BF16)
    woa = jax.vmap(_kh_rows)(P['w_out'][:, :A_WIDTH]).astype(_BF16)
    wob = P['w_out'][:, A_WIDTH:].astype(_BF16)
    return dict(wta=wta, wb=wb, woa=woa, wob=wob,
                w_up=P['w_up'].astype(_BF16), w_down=P['w_down'].astype(_BF16))


def _layer_weights(P, l, tb_prep, tb_post):
    vb = _kh_rows(P['vres_bias'][l - 1]) if l > 0 else jnp.zeros((A_WIDTH,), _F32)
    prm = jnp.stack([_kh_rows(P['w0'][l]), _kh_rows(P['a0'][l]), _kh_rows(P['k_k'][l]),
                     _kh_rows(P['k_a'][l]), _kh_rows(P['r_k'][l].reshape(A_WIDTH)), vb])
    prm = jnp.broadcast_to(prm[:, :, None], (6, A_WIDTH, tb_prep))
    if l > 0:
        v1 = _kh_rows(P['vres_down'][l - 1]).T.astype(_BF16)
        v2 = _kh_rows(P['vres_up'][l - 1].T).astype(_BF16)
    else:
        v1 = jnp.zeros((A_MV_LORA, A_WIDTH), _BF16)
        v2 = jnp.zeros((A_WIDTH, A_MV_LORA), _BF16)
    return dict(
        mu=_bcast_cols(_rwkv_rows(P['tok_mu'][l]), tb_prep), prm=prm,
        wup=_kh_rows(P['w_lora_up'][l].T).astype(_BF16),
        aup=_kh_rows(P['a_lora_up'][l].T).astype(_BF16),
        gup=_kh_rows(P['g_lora_up'][l].T).astype(_BF16),
        v1=v1, v2=v2,
        lng=_bcast_cols(_kh_rows(P['lnx_g'][l]), tb_post),
        lnb=_bcast_cols(_kh_rows(P['lnx_b'][l]), tb_post),
        fup=P['gla_f_up'][l].astype(_BF16), fb=P['gla_f_bias'][l][None], ng=P['gla_norm_g'][l][None],
        ln1g=P['ln1_g'][l][None], ln1b=P['ln1_b'][l][None],
        cw=P['conv_w'][l], cb=P['conv_b'][l][None],
        ln2g=P['ln2_g'][l][None], ln2b=P['ln2_b'][l][None])


def _trunk(x3, states, P, SW, prompt):
    n_seq, seq_len, _ = x3.shape
    n = n_seq * seq_len
    tb = 2 * LANES
    if prompt:
        x = x3.reshape(n, D_MODEL)
        scan_tb, gla_chunk, gla_valid, gla_t, tm_out, tm_ffn = LANES, GLA_CHUNK, GLA_CHUNK, seq_len, 256, 512
        st_gla = None
    else:
        assert n_seq == LANES
        x = jnp.swapaxes(x3, 0, 1).reshape(n, D_MODEL)
        st_rwkv, st_shift, st_gla, st_conv = states
        gla_chunk, gla_valid, gla_t, tm_out, tm_ffn = SUBLANES, seq_len, SUBLANES, 256, (CONV_W - 1) * n_seq
    new_rwkv, new_shift, new_gla, new_conv = [], [], [], []
    ops0 = None
    for l in range(DEPTH):
        W = _layer_weights(P, l, tb, tm_out)
        if prompt:
            bnd = None
        else:
            bnd = _rwkv_rows(st_shift[l][:, 0, :].T)
        ops, g, bonus, pb, pa_out = _prep(
            x, SW['wta'], SW['wb'], l, bnd, ops0, W['mu'], W['prm'], W['wup'], W['aup'], W['gup'],
            W['v1'], W['v2'], n_seq=n_seq, seq_len=seq_len, tb=tb)
        if l == 0:
            ops0 = ops
        if prompt:
            o_cm, s_fin = _scan_cm(ops, seq_len, scan_tb)
            s_fin = _lanes_to_state(s_fin)
            last_cols = pa_out[:, :, tb - 1].T
        else:
            o_hv, s_fin = _scan_short(st_rwkv, l, ops, n_seq)
            o_cm = jnp.swapaxes(o_hv, 0, 1).reshape(A_WIDTH, n)
            last_cols = pa_out[:, n - n_seq:]
        if prompt:
            pb3 = pb.reshape(n_seq, seq_len, GLA_IN_PAD)
        else:
            pb3 = jnp.swapaxes(pb.reshape(seq_len, n_seq, GLA_IN_PAD), 0, 1)
            pb3 = jnp.pad(pb3, ((0, 0), (0, gla_t - seq_len), (0, 0)))
        ob, s_gla = _gla(pb3, st_gla, l, W['fup'], W['fb'], W['ng'],
                         nb=SCAN_GROUP, chunk=gla_chunk, valid=gla_valid)
        if prompt:
            ob = ob.reshape(n, B_WIDTH)
        else:
            ob = jnp.swapaxes(ob[:, :seq_len], 0, 1).reshape(n, B_WIDTH)
        x1 = _outproj(x, o_cm, g, bonus, W['lng'], W['lnb'], ob, SW['woa'], SW['wob'], l,
                      W['ln1g'], W['ln1b'], tm_out)
        if prompt:
            past = None
        else:
            past = jnp.swapaxes(st_conv[l], 0, 1).reshape((CONV_W - 1) * n_seq, D_FF)
        x, gate = _ffn(x1, past, SW['w_up'], W['cw'], W['cb'], SW['w_down'], l,
                       W['ln2g'], W['ln2b'], n_seq=n_seq, seq_len=seq_len, tm=tm_ffn)
        new_rwkv.append(s_fin)
        new_shift.append(_rwkv_rows_inv(last_cols).T[:, None, :])
        new_gla.append(s_gla)
        if prompt:
            new_conv.append(gate[:, SUBLANES - (CONV_W - 1):])
        else:
            tail = gate[n - (CONV_W - 1) * n_seq:].reshape(CONV_W - 1, n_seq, D_FF)
            new_conv.append(jnp.swapaxes(tail, 0, 1))
    if prompt:
        y = x.reshape(n_seq, seq_len, D_MODEL)
    else:
        y = jnp.swapaxes(x.reshape(seq_len, n_seq, D_MODEL), 0, 1)
    return (y, jnp.stack(new_rwkv), jnp.stack(new_shift), jnp.stack(new_gla), jnp.stack(new_conv))


def kernel(x_prompt, x_sample, state_rwkv, state_shift, state_gla, state_conv, w_in, tok_mu, w0, w_lora_up, a0, a_lora_up, g_lora_up, k_k, k_a, r_k, lnx_g, lnx_b, vres_bias, vres_down, vres_up, gla_f_up, gla_f_bias, gla_norm_g, w_out, ln1_g, ln1_b, w_up, conv_w, conv_b, w_down, ln2_g, ln2_b):
    P = dict(w_in=w_in, tok_mu=tok_mu, w0=w0, w_lora_up=w_lora_up, a0=a0, a_lora_up=a_lora_up,
             g_lora_up=g_lora_up, k_k=k_k, k_a=k_a, r_k=r_k, lnx_g=lnx_g, lnx_b=lnx_b,
             vres_bias=vres_bias, vres_down=vres_down, vres_up=vres_up, gla_f_up=gla_f_up,
             gla_f_bias=gla_f_bias, gla_norm_g=gla_norm_g, w_out=w_out, ln1_g=ln1_g, ln1_b=ln1_b,
             w_up=w_up, conv_w=conv_w, conv_b=conv_b, w_down=w_down, ln2_g=ln2_g, ln2_b=ln2_b)
    SW = _stacked_weights(P)
    y_p, rwkv_p, shift_p, gla_p, conv_p = _trunk(x_prompt, None, P, SW, True)
    y_s, rwkv_s, shift_s, gla_s, conv_s = _trunk(
        x_sample, (state_rwkv, state_shift, state_gla, state_conv), P, SW, False)
    return (y_p, y_s, rwkv_p, rwkv_s, shift_p, shift_s, gla_p, gla_s, conv_p, conv_s)
```
